```python
import math
import jax, jax.numpy as jnp
from jax import lax
import numpy as np

D_MODEL = 1024
BATCH = 16
SEQ = 2048
DEPTH = 2

CHUNK = 64
Q_BLOCK = 128
EPS = 1e-6
A_HEADS = 4
A_QK_DIM = 32
A_V_DIM = 64
B_HEADS = 6
B_DIM = 64
CONV_WIDTH = 4
C_HEADS = 6
C_Q_LORA = 192
C_KV_LORA = 128
C_NOPE = 64
C_ROPE = 32
C_V_DIM = 64
ROPE_THETA = 10000.0
D_MIX = A_HEADS * A_V_DIM + B_HEADS * B_DIM + C_HEADS * C_V_DIM
IN_SIZES = (A_HEADS * A_QK_DIM, A_HEADS * A_QK_DIM, A_HEADS * A_QK_DIM, A_HEADS * A_QK_DIM, A_HEADS * A_V_DIM,
            B_HEADS * B_DIM, B_HEADS * B_DIM, B_HEADS * B_DIM, B_HEADS * B_DIM, B_HEADS, B_HEADS,
            C_Q_LORA, C_KV_LORA, C_ROPE)
IN_WIDTH = sum(IN_SIZES)
D_FF = 3584
N_EXPERTS = 8
TOP_K = 2
EXPERT_BLOCK = 256

kernel_name = "hybrid_hymba_diff_mlstm_mla_moe_block"


def rmsnorm(u, g):
    uf = u.astype(jnp.float32)
    r = uf * lax.rsqrt(jnp.mean(uf * uf, axis=-1, keepdims=True) + EPS)
    return (r * g.astype(jnp.float32)).astype(u.dtype)


def ada_modulation(c, w, b):
    p = jax.nn.silu(c) @ w + b
    shift, scale, gate = jnp.split(p, 3, axis=-1)
    return shift[:, None, :], scale[:, None, :], gate[:, None, :]


def chunk_mask(s0, s1):
    qc = (s0 + jnp.arange(s1 - s0)) // CHUNK
    kc = jnp.arange(s1) // CHUNK
    return kc[None, :] <= qc[:, None]


def masked_softmax(s, mask, scale):
    s = jnp.where(mask, s.astype(jnp.float32) * scale, -jnp.inf)
    return jax.nn.softmax(s, axis=-1)


def sweep_query_blocks(block_fn, seq):
    return jnp.concatenate([block_fn(s0, s0 + Q_BLOCK) for s0 in range(0, seq, Q_BLOCK)], axis=1)


def rope(u, cos, sin):
    half = u.shape[-1] // 2
    uf = u.astype(jnp.float32)
    u1, u2 = uf[..., :half], uf[..., half:]
    return jnp.concatenate([u1 * cos - u2 * sin, u1 * sin + u2 * cos], axis=-1).astype(u.dtype)


def causal_conv(u, w, b):
    k = w.shape[0]
    s = u.shape[1]
    up = jnp.pad(u, ((0, 0), (k - 1, 0), (0, 0)))
    return sum(up[:, j:j + s] * w[j] for j in range(k)) + b


def diff_attention_group(q1, q2, k1, k2, v, lam_q1, lam_k1, lam_q2, lam_k2, g_sub, lam_init):
    bsz, s, _ = q1.shape
    q1 = q1.reshape(bsz, s, A_HEADS, A_QK_DIM)
    q2 = q2.reshape(bsz, s, A_HEADS, A_QK_DIM)
    k1 = k1.reshape(bsz, s, A_HEADS, A_QK_DIM)
    k2 = k2.reshape(bsz, s, A_HEADS, A_QK_DIM)
    v = v.reshape(bsz, s, A_HEADS, A_V_DIM)
    f32 = jnp.float32
    lam = (jnp.exp(jnp.sum(lam_q1.astype(f32) * lam_k1.astype(f32)))
           - jnp.exp(jnp.sum(lam_q2.astype(f32) * lam_k2.astype(f32))) + lam_init)
    scale = A_QK_DIM ** -0.5

    def block(s0, s1):
        mask = chunk_mask(s0, s1)
        p1 = masked_softmax(jnp.einsum('bqhd,bkhd->bhqk', q1[:, s0:s1], k1[:, :s1]), mask, scale)
        p2 = masked_softmax(jnp.einsum('bqhd,bkhd->bhqk', q2[:, s0:s1], k2[:, :s1]), mask, scale)
        a = (p1 - lam * p2).astype(v.dtype)
        return jnp.einsum('bhqk,bkhd->bqhd', a, v[:, :s1])

    o = sweep_query_blocks(block, s)
    o = rmsnorm(o, g_sub) * (1.0 - lam_init)
    return o.reshape(bsz, s, A_HEADS * A_V_DIM)


def mlstm_group(q, k, v, o, ig_pre, fg_pre, w_conv, b_conv, b_i, b_f, g_out):
    bsz, s, _ = q.shape
    f32 = jnp.float32
    qk = jax.nn.silu(causal_conv(jnp.concatenate([q, k], axis=-1), w_conv, b_conv))
    q, k = jnp.split(qk, 2, axis=-1)
    nc = s // CHUNK

    def to_chunks(u):
        return u.astype(f32).reshape(bsz, nc, CHUNK, B_HEADS, B_DIM).transpose(1, 0, 3, 2, 4)

    def gate_chunks(g):
        return g.astype(f32).reshape(bsz, nc, CHUNK, B_HEADS).transpose(1, 0, 3, 2)

    qc, kc, vc = to_chunks(q), to_chunks(k) * (B_DIM ** -0.5), to_chunks(v)
    igc = gate_chunks(ig_pre.astype(f32) + b_i.astype(f32))
    lfc = jax.nn.log_sigmoid(gate_chunks(fg_pre.astype(f32) + b_f.astype(f32)))
    tril = jnp.tril(jnp.ones((CHUNK, CHUNK), dtype=bool))

    def step(carry, inp):
        cmat, nvec, m = carry
        qq, kk, vv, ig, lf = inp
        b = jnp.cumsum(lf, axis=-1)
        d = jnp.where(tril, b[..., :, None] - b[..., None, :] + ig[..., None, :], -jnp.inf)
        inter = b + m[..., None]
        m_t = jnp.maximum(inter, d.max(axis=-1))
        w_intra = jnp.exp(d - m_t[..., None])
        w_inter = jnp.exp(inter - m_t)
        sc = jnp.einsum('bhtd,bhsd->bhts', qq, kk) * w_intra
        num = jnp.einsum('bhts,bhsd->bhtd', sc, vv) + w_inter[..., None] * jnp.einsum('bhed,bhtd->bhte', cmat, qq)
        den = sc.sum(axis=-1) + w_inter * jnp.einsum('bhd,bhtd->bht', nvec, qq)
        h = num / jnp.maximum(jnp.abs(den), jnp.exp(-m_t))[..., None]
        b_last = b[..., -1]
        g = b_last[..., None] - b + ig
        m_new = jnp.maximum(b_last + m, g.max(axis=-1))
        w_s = jnp.exp(g - m_new[..., None])
        w_c = jnp.exp(b_last + m - m_new)
        cmat = w_c[..., None, None] * cmat + jnp.einsum('bhs,bhse,bhsd->bhed', w_s, vv, kk)
        nvec = w_c[..., None] * nvec + jnp.einsum('bhs,bhsd->bhd', w_s, kk)
        return (cmat, nvec, m_new), h

    init = (jnp.zeros((bsz, B_HEADS, B_DIM, B_DIM), f32), jnp.zeros((bsz, B_HEADS, B_DIM), f32),
            jnp.zeros((bsz, B_HEADS), f32))
    _, hc = lax.scan(step, init, (qc, kc, vc, igc, lfc))
    h = hc.transpose(1, 0, 3, 2, 4).reshape(bsz, s, B_HEADS, B_DIM)
    h = rmsnorm(h, g_out.reshape(B_HEADS, B_DIM)).reshape(bsz, s, B_HEADS * B_DIM)
    return h.astype(o.dtype) * jax.nn.sigmoid(o)


def mla_group(c_q, c_kv, k_rope, g_q_lat, g_kv_lat, w_uq, w_ukv, cos, sin):
    bsz, s, _ = c_q.shape
    q = (rmsnorm(c_q, g_q_lat) @ w_uq).reshape(bsz, s, C_HEADS, C_NOPE + C_ROPE)
    kv = (rmsnorm(c_kv, g_kv_lat) @ w_ukv).reshape(bsz, s, C_HEADS, C_NOPE + C_V_DIM)
    q_nope, q_rope = q[..., :C_NOPE], rope(q[..., C_NOPE:], cos[:, :, None, :], sin[:, :, None, :])
    k_nope, v = kv[..., :C_NOPE], kv[..., C_NOPE:]
    k_rope = rope(k_rope, cos, sin)
    scale = (C_NOPE + C_ROPE) ** -0.5

    def block(s0, s1):
        mask = chunk_mask(s0, s1)
        sc = (jnp.einsum('bqhd,bkhd->bhqk', q_nope[:, s0:s1], k_nope[:, :s1])
              + jnp.einsum('bqhr,bkr->bhqk', q_rope[:, s0:s1], k_rope[:, :s1]))
        p = masked_softmax(sc, mask, scale).astype(v.dtype)
        return jnp.einsum('bhqk,bkhd->bqhd', p, v[:, :s1])

    return sweep_query_blocks(block, s).reshape(bsz, s, C_HEADS * C_V_DIM)


def hybrid_mixer(h, positions, layer, w_in, w_out, lam_q1, lam_k1, lam_q2, lam_k2, g_diff,
                 w_conv, b_conv, b_igate, b_fgate, g_mlstm, g_q_lat, g_kv_lat, w_uq, w_ukv):
    split_idx = np.cumsum(IN_SIZES)[:-1].tolist()
    (q1, q2, k1, k2, va, qb, kb, vb, ob, ib, fb, cq, ckv, kr) = jnp.split(h @ w_in, split_idx, axis=-1)
    lam_init = 0.8 - 0.6 * math.exp(-0.3 * layer)
    out_a = diff_attention_group(q1, q2, k1, k2, va, lam_q1, lam_k1, lam_q2, lam_k2, g_diff, lam_init)
    out_b = mlstm_group(qb, kb, vb, ob, ib, fb, w_conv, b_conv, b_igate, b_fgate, g_mlstm)
    freqs = ROPE_THETA ** (-jnp.arange(0, C_ROPE, 2, dtype=jnp.float32) / C_ROPE)
    ang = positions.astype(jnp.float32)[..., None] * freqs
    out_c = mla_group(cq, ckv, kr, g_q_lat, g_kv_lat, w_uq, w_ukv, jnp.cos(ang), jnp.sin(ang))
    return jnp.concatenate([out_a, out_b, out_c], axis=-1) @ w_out


def swiglu(u, wg, wu, wd):
    return (jax.nn.silu(u @ wg) * (u @ wu)) @ wd


def moe_swiglu(h, w_router, b_router, w_gate_e, w_up_e, w_down_e):
    bsz, s, d = h.shape
    n = bsz * s
    t = h.reshape(n, d)
    logits = (t @ w_router).astype(jnp.float32) + b_router.astype(jnp.float32)
    top_val, top_idx = lax.top_k(logits, TOP_K)
    gates = jax.nn.softmax(top_val, axis=-1)
    e_flat = top_idx.reshape(-1)
    tok_flat = jnp.repeat(jnp.arange(n, dtype=jnp.int32), TOP_K)
    g_flat = gates.reshape(-1)
    order = jnp.argsort(e_flat, stable=True)
    e_sorted, tok_sorted, g_sorted = e_flat[order], tok_flat[order], g_flat[order]
    counts = jnp.zeros((N_EXPERTS,), jnp.int32).at[e_flat].add(1)
    starts = jnp.cumsum(counts) - counts
    padded = ((counts + EXPERT_BLOCK - 1) // EXPERT_BLOCK) * EXPERT_BLOCK
    pend = jnp.cumsum(padded)
    pstarts = pend - padded
    dest = pstarts[e_sorted] + (jnp.arange(n * TOP_K, dtype=jnp.int32) - starts[e_sorted])
    n_blocks = -(-(n * TOP_K) // EXPERT_BLOCK) + N_EXPERTS
    p = n_blocks * EXPERT_BLOCK
    buf_tok = jnp.full((p,), n, jnp.int32).at[dest].set(tok_sorted)
    t_pad = jnp.concatenate([t, jnp.zeros((1, d), t.dtype)], axis=0)
    xb = t_pad[buf_tok].reshape(n_blocks, EXPERT_BLOCK, d)
    blk_e = jnp.minimum(jnp.searchsorted(pend, jnp.arange(n_blocks) * EXPERT_BLOCK, side='right'), N_EXPERTS - 1)

    def expert_block(args):
        xblk, e = args
        return swiglu(xblk, w_gate_e[e], w_up_e[e], w_down_e[e])

    yb = lax.map(expert_block, (xb, blk_e)).reshape(p, d)
    y = jnp.zeros((n, d), h.dtype).at[tok_sorted].add(yb[dest] * g_sorted[:, None].astype(h.dtype))
    return y.reshape(bsz, s, d)


def setup_inputs(seed: int = 0) -> dict:
    key = jax.random.key(seed)
    ks = jax.random.split(key, 32)
    f32 = jnp.float32
    n_dense = (DEPTH + 1) // 2
    n_moe = DEPTH // 2

    def nrm(k, shape, scale):
        return jax.random.normal(k, shape, f32) * scale

    def gain(k, shape):
        return 1.0 + 0.05 * jax.random.normal(k, shape, f32)

    offsets = jax.random.randint(ks[2], (BATCH, 1), 0, 64, dtype=jnp.int32) * CHUNK
    positions = offsets + jnp.arange(SEQ, dtype=jnp.int32)[None, :]
    b_fgate = jnp.linspace(3.0, 6.0, B_HEADS, dtype=f32)[None, :] + 0.1 * jax.random.normal(ks[15], (DEPTH, B_HEADS), f32)
    return {
        "x": nrm(ks[0], (BATCH, SEQ, D_MODEL), 1.0),
        "c": nrm(ks[1], (BATCH, D_MODEL), 1.0),
        "positions": positions,
        "w_ada": nrm(ks[3], (DEPTH, 2, D_MODEL, 3 * D_MODEL), D_MODEL ** -0.5),
        "b_ada": nrm(ks[4], (DEPTH, 2, 3 * D_MODEL), 0.01),
        "g_norm": gain(ks[5], (DEPTH, 4, D_MODEL)),
        "w_in": nrm(ks[6], (DEPTH, D_MODEL, IN_WIDTH), D_MODEL ** -0.5),
        "w_out": nrm(ks[7], (DEPTH, D_MIX, D_MODEL), D_MIX ** -0.5),
        "lam_q1": nrm(ks[8], (DEPTH, A_QK_DIM), 0.1),
        "lam_k1": nrm(ks[9], (DEPTH, A_QK_DIM), 0.1),
        "lam_q2": nrm(ks[10], (DEPTH, A_QK_DIM), 0.1),
        "lam_k2": nrm(ks[11], (DEPTH, A_QK_DIM), 0.1),
        "g_diff": gain(ks[12], (DEPTH, A_V_DIM)),
        "w_conv": nrm(ks[13], (DEPTH, CONV_WIDTH, 2 * B_HEADS * B_DIM), CONV_WIDTH ** -0.5),
        "b_conv": nrm(ks[14], (DEPTH, 2 * B_HEADS * B_DIM), 0.01),
        "b_igate": nrm(ks[16], (DEPTH, B_HEADS), 0.1),
        "b_fgate": b_fgate,
        "g_mlstm": gain(ks[17], (DEPTH, B_HEADS * B_DIM)),
        "g_q_lat": gain(ks[18], (DEPTH, C_Q_LORA)),
        "g_kv_lat": gain(ks[19], (DEPTH, C_KV_LORA)),
        "w_uq": nrm(ks[20], (DEPTH, C_Q_LORA, C_HEADS * (C_NOPE + C_ROPE)), C_Q_LORA ** -0.5),
        "w_ukv": nrm(ks[21], (DEPTH, C_KV_LORA, C_HEADS * (C_NOPE + C_V_DIM)), C_KV_LORA ** -0.5),
        "w_gate_d": nrm(ks[22], (n_dense, D_MODEL, D_FF), D_MODEL ** -0.5),
        "w_up_d": nrm(ks[23], (n_dense, D_MODEL, D_FF), D_MODEL ** -0.5),
        "w_down_d": nrm(ks[24], (n_dense, D_FF, D_MODEL), D_FF ** -0.5),
        "w_router": nrm(ks[25], (n_moe, D_MODEL, N_EXPERTS), D_MODEL ** -0.5),
        "b_router": nrm(ks[26], (n_moe, N_EXPERTS), 0.01),
        "w_gate_e": nrm(ks[27], (n_moe, N_EXPERTS, D_MODEL, D_FF), D_MODEL ** -0.5),
        "w_up_e": nrm(ks[28], (n_moe, N_EXPERTS, D_MODEL, D_FF), D_MODEL ** -0.5),
        "w_down_e": nrm(ks[29], (n_moe, N_EXPERTS, D_FF, D_MODEL), D_FF ** -0.5),
    }


def reference(x, c, positions, w_ada, b_ada, g_norm, w_in, w_out, lam_q1, lam_k1, lam_q2, lam_k2, g_diff,
              w_conv, b_conv, b_igate, b_fgate, g_mlstm, g_q_lat, g_kv_lat, w_uq, w_ukv,
              w_gate_d, w_up_d, w_down_d, w_router, b_router, w_gate_e, w_up_e, w_down_e):
    for l in range(DEPTH):
        shift, scale, gate = ada_modulation(c, w_ada[l, 0], b_ada[l, 0])
        h = rmsnorm(x, g_norm[l, 0]) * (1.0 + scale) + shift
        y = hybrid_mixer(h, positions, l, w_in[l], w_out[l], lam_q1[l], lam_k1[l], lam_q2[l], lam_k2[l], g_diff[l],
                         w_conv[l], b_conv[l], b_igate[l], b_fgate[l], g_mlstm[l], g_q_lat[l], g_kv_lat[l],
                         w_uq[l], w_ukv[l])
        x = x + gate * rmsnorm(y, g_norm[l, 1])
        shift, scale, gate = ada_modulation(c, w_ada[l, 1], b_ada[l, 1])
        h = rmsnorm(x, g_norm[l, 2]) * (1.0 + scale) + shift
        if l % 2 == 0:
            y = swiglu(h, w_gate_d[l // 2], w_up_d[l // 2], w_down_d[l // 2])
        else:
            y = moe_swiglu(h, w_router[l // 2], b_router[l // 2], w_gate_e[l // 2], w_up_e[l // 2], w_down_e[l // 2])
        x = x + gate * rmsnorm(y, g_norm[l, 3])
    return x
```

```python
import functools
import math

import jax
import jax.numpy as jnp
import numpy as np
from jax import lax
from jax.experimental import pallas as pl
from jax.experimental.pallas import tpu as pltpu

F32 = jnp.float32
BF16 = jnp.bfloat16
HIGHEST = lax.Precision.HIGHEST

D_MODEL = 1024
DEPTH = 2
CHUNK = 64
EPS = 1e-6
A_HEADS, A_QK_DIM, A_V_DIM = 4, 32, 64
B_HEADS, B_DIM, CONV_WIDTH = 6, 64, 4
C_HEADS, C_Q_LORA, C_KV_LORA, C_NOPE, C_ROPE, C_V_DIM = 6, 192, 128, 64, 32, 64
ROPE_THETA = 10000.0
D_FF = 3584
N_EXPERTS = 8
LOG2E = 1.4426950408889634
NEG_BIG = -1e30

LANES = 128
DIFF_W = 2 * A_HEADS * A_QK_DIM
MLSTM_W = B_HEADS * B_DIM
MLA_SLAB = 128
MLA_QK_W = C_HEADS * MLA_SLAB
MLA_V_W = C_HEADS * C_V_DIM

COL_DIFF = 0
COL_MLSTM = 768
COL_CQ = 2304
COL_CKV = 2560
COL_X = 2688
COL_Y = 2816
IN_COLS = 2944

VMEM_LIMIT = 56 * 1024 * 1024


def _cparams(sem):
    return pltpu.CompilerParams(dimension_semantics=sem, vmem_limit_bytes=VMEM_LIMIT)


def _nt(a, b):
    return lax.dot_general(a, b, (((1,), (1,)), ((), ())), preferred_element_type=F32)


def _nn(a, b):
    return jnp.dot(a, b, preferred_element_type=F32)


def _sigmoid(v):
    return 1.0 / (1.0 + jnp.exp(-v))


def _rms(v, n):
    return v * lax.rsqrt(jnp.sum(v * v, axis=-1, keepdims=True) * (1.0 / n) + EPS)


def _ada_body(c_ref, w_ref, b_ref, o_ref):
    c = c_ref[...]
    sc = c * _sigmoid(c)
    o_ref[0, 0] = jnp.dot(sc, w_ref[0], precision=HIGHEST, preferred_element_type=F32) + b_ref[0]


def _ada_all(c, w_ada, b_ada):
    bsz, d = c.shape
    n = w_ada.shape[0] * w_ada.shape[1]
    w = w_ada.reshape(n, d, 3 * d)
    b = b_ada.reshape(n, 1, 3 * d)
    return pl.pallas_call(
        _ada_body,
        grid=(n, 3),
        in_specs=[pl.BlockSpec((bsz, d), lambda i, j: (0, 0)),
                  pl.BlockSpec((1, d, d), lambda i, j: (i, 0, j)),
                  pl.BlockSpec((1, 1, d), lambda i, j: (i, 0, j))],
        out_specs=pl.BlockSpec((1, 1, bsz, d), lambda i, j: (i, j, 0, 0)),
        out_shape=jax.ShapeDtypeStruct((n, 3, bsz, d), F32),
        compiler_params=_cparams(("arbitrary", "arbitrary")),
        name="ada_mod",
    )(c, w, b)


def _mix_in_body(tiles_per_seq, tm,
                 x_ref, shift_ref, scale_ref, g_ref, win_ref, wconv_ref, bconv_ref, gq_ref, gkv_ref,
                 wq_ref, wqr_ref, wkv_ref, cos_ref, sin_ref,
                 qd_ref, kd_ref, vd_ref, qm_ref, km_ref, vm_ref, om_ref, gt_ref, qc_ref, kc_ref, vc_ref,
                 carry_ref):
    r = pl.program_id(0)
    x = x_ref[...]
    h = _rms(x, D_MODEL) * g_ref[...]
    h = h * (1.0 + scale_ref[0]) + shift_ref[0]
    p = _nn(h.astype(BF16), win_ref[...])

    qd_ref[...] = (p[:, 0:256] * (A_QK_DIM ** -0.5 * LOG2E)).astype(BF16)
    kd_ref[...] = p[:, 256:512].astype(BF16)
    vd_ref[...] = p[:, 512:768].astype(BF16)

    @pl.when(r % tiles_per_seq == 0)
    def _():
        carry_ref[0:8, :] = jnp.zeros((8, 2 * MLSTM_W), F32)

    carry_ref[8:8 + tm, :] = p[:, COL_MLSTM:COL_MLSTM + 2 * MLSTM_W]
    w = wconv_ref[...]
    conv = bconv_ref[...] + carry_ref[8:8 + tm, :] * w[3:4, :]
    for j in range(CONV_WIDTH - 1):
        conv = conv + carry_ref[5 + j:5 + j + tm, :] * w[j:j + 1, :]
    carry_ref[0:8, :] = carry_ref[tm:tm + 8, :]
    qk = conv * _sigmoid(conv)
    qm_ref[...] = qk[:, 0:MLSTM_W].astype(BF16)
    km_ref[...] = (qk[:, MLSTM_W:2 * MLSTM_W] * (B_DIM ** -0.5)).astype(BF16)
    vm_ref[...] = p[:, COL_MLSTM + 2 * MLSTM_W:COL_MLSTM + 3 * MLSTM_W].astype(BF16)
    om_ref[...] = p[:, COL_MLSTM + 3 * MLSTM_W:COL_MLSTM + 4 * MLSTM_W].astype(BF16)

    xg = p[:, COL_X:COL_X + LANES]
    gt_ref[0] = xg.T[0:16, :]

    cq = p[:, COL_CQ:COL_CQ + 256]
    cqn = (cq * lax.rsqrt(jnp.sum(cq * cq, axis=-1, keepdims=True) * (1.0 / C_Q_LORA) + EPS) * gq_ref[...]).astype(BF16)
    ckv = p[:, COL_CKV:COL_CKV + C_KV_LORA]
    ckvn = (_rms(ckv, C_KV_LORA) * gkv_ref[...]).astype(BF16)
    cos_t = cos_ref[...]
    sin_t = sin_ref[...]
    cos6 = jnp.concatenate([cos_t] * C_HEADS, axis=1)
    sin6 = jnp.concatenate([sin_t] * C_HEADS, axis=1)
    qn = _nn(cqn, wq_ref[...])
    qr = _nn(cqn, wqr_ref[...])
    qc_ref[...] = ((qn * cos6 + qr * sin6) * ((C_NOPE + C_ROPE) ** -0.5 * LOG2E)).astype(BF16)
    kvp = _nn(ckvn, wkv_ref[...])
    lane = lax.broadcasted_iota(jnp.int32, (1, LANES), 1)
    rope_lanes = (lane >= C_NOPE) & (lane < C_NOPE + C_ROPE)
    yg = p[:, COL_Y:COL_Y + LANES]
    krope = jnp.where(rope_lanes, xg * cos_t + yg * sin_t, 0.0)
    kc_ref[...] = (kvp[:, 0:MLA_QK_W] + jnp.concatenate([krope] * C_HEADS, axis=1)).astype(BF16)
    vc_ref[...] = kvp[:, MLA_QK_W:MLA_QK_W + MLA_V_W].astype(BF16)


def _mix_in(x2, shift, scale, g, win, wconv, bconv, gq, gkv, wq, wqr, wkv, cos_t, sin_t, bsz, seq):
    n = bsz * seq
    tm = min(512, seq)
    tps = seq // tm
    d = D_MODEL
    row = lambda w: pl.BlockSpec((tm, w), lambda r: (r, 0))
    full = lambda a: pl.BlockSpec(a.shape, lambda r: (0,) * a.ndim)
    mod = pl.BlockSpec((1, 1, d), lambda r: (r // tps, 0, 0))
    out_widths = [DIFF_W, DIFF_W, DIFF_W, MLSTM_W, MLSTM_W, MLSTM_W, MLSTM_W]
    out_shape = [jax.ShapeDtypeStruct((n, w), BF16) for w in out_widths]
    out_specs = [row(w) for w in out_widths]
    out_shape.append(jax.ShapeDtypeStruct((bsz, 16, seq), F32))
    out_specs.append(pl.BlockSpec((1, 16, tm), lambda r: (r // tps, 0, r % tps)))
    for w in (MLA_QK_W, MLA_QK_W, MLA_V_W):
        out_shape.append(jax.ShapeDtypeStruct((n, w), BF16))
        out_specs.append(row(w))
    return pl.pallas_call(
        functools.partial(_mix_in_body, tps, tm),
        grid=(n // tm,),
        in_specs=[row(d), mod, mod, full(g), full(win), full(wconv), full(bconv), full(gq), full(gkv),
                  full(wq), full(wqr), full(wkv), row(LANES), row(LANES)],
        out_specs=out_specs,
        out_shape=out_shape,
        scratch_shapes=[pltpu.VMEM((tm + 8, 2 * MLSTM_W), F32)],
        compiler_params=_cparams(("arbitrary",)),
        name="mix_in",
    )(x2, shift, scale, g, win, wconv, bconv, gq, gkv, wq, wqr, wkv, cos_t, sin_t)


def _chunk_mask(rows, tq, tk):
    ri = lax.broadcasted_iota(jnp.int32, (rows, tk), 0)
    ci = lax.broadcasted_iota(jnp.int32, (rows, tk), 1)
    return (ci >> 6) <= ((ri & (tq - 1)) >> 6)


def _flash_loop(i, tk, score_fn, pv_fn, mask, m_ref, l_ref):
    m_ref[...] = jnp.full(m_ref.shape, NEG_BIG, F32)
    l_ref[...] = jnp.zeros(l_ref.shape, F32)

    def step(j, masked):
        s = score_fn(j)
        if masked:
            s = jnp.where(mask, s, NEG_BIG)
        m_old = m_ref[...]
        m_new = jnp.maximum(m_old, jnp.max(s, axis=1, keepdims=True))
        alpha = jnp.exp2(m_old - m_new)
        pr = jnp.exp2(s - m_new)
        l_ref[...] = alpha * l_ref[...] + jnp.sum(pr, axis=1, keepdims=True)
        m_ref[...] = m_new
        pv_fn(j, alpha, pr.astype(BF16))

    def body(j, c):
        step(j, False)
        return c

    lax.fori_loop(0, i, body, 0)
    step(i, True)


def _diff_attn_body(tq, lam_init, q_ref, k_ref, v_ref, lam_ref, g_ref, o_ref, m_ref, l_ref, acc_ref):
    i = pl.program_id(1)
    tk = tq
    nslab = 2 * A_HEADS
    q = q_ref[...]
    lane = lax.broadcasted_iota(jnp.int32, (1, DIFF_W), 1)
    zero = jnp.zeros_like(q)
    qs = jnp.concatenate([jnp.where((lane >= A_QK_DIM * s) & (lane < A_QK_DIM * (s + 1)), q, zero)
                          for s in range(nslab)], axis=0)
    acc_ref[...] = jnp.zeros(acc_ref.shape, F32)
    mask = _chunk_mask(nslab * tq, tq, tk)

    def score(j):
        return _nt(qs, k_ref[pl.ds(pl.multiple_of(j * tk, tk), tk), :])

    def pv(j, alpha, pb):
        acc_ref[...] = alpha * acc_ref[...] + _nn(pb, v_ref[pl.ds(pl.multiple_of(j * tk, tk), tk), :])

    _flash_loop(i, tk, score, pv, mask, m_ref, l_ref)

    lv = lam_ref[...]
    lam = (jnp.exp(jnp.sum(lv[0:1] * lv[1:2], axis=1, keepdims=True))
           - jnp.exp(jnp.sum(lv[2:3] * lv[3:4], axis=1, keepdims=True)) + lam_init)
    o = acc_ref[...] / l_ref[...]
    out = jnp.zeros((tq, DIFF_W), F32)
    for hd in range(A_HEADS):
        sel = (lane >= A_V_DIM * hd) & (lane < A_V_DIM * (hd + 1))
        oh = o[hd * tq:(hd + 1) * tq] - lam * o[(A_HEADS + hd) * tq:(A_HEADS + hd + 1) * tq]
        ms = jnp.sum(jnp.where(sel, oh * oh, 0.0), axis=1, keepdims=True) * (1.0 / A_V_DIM)
        out = out + jnp.where(sel, oh * lax.rsqrt(ms + EPS), 0.0)
    o_ref[...] = (out * g_ref[...] * (1.0 - lam_init)).astype(BF16)


def _diff_attn(qd, kd, vd, lamv, g4, lam_init, bsz, seq):
    tq = min(256, seq)
    nq = seq // tq
    nslab = 2 * A_HEADS
    return pl.pallas_call(
        functools.partial(_diff_attn_body, tq, lam_init),
        grid=(bsz, nq),
        in_specs=[pl.BlockSpec((tq, DIFF_W), lambda b, i: (b * nq + i, 0)),
                  pl.BlockSpec((seq, DIFF_W), lambda b, i: (b, 0)),
                  pl.BlockSpec((seq, DIFF_W), lambda b, i: (b, 0)),
                  pl.BlockSpec(lamv.shape, lambda b, i: (0, 0)),
                  pl.BlockSpec(g4.shape, lambda b, i: (0, 0))],
        out_specs=pl.BlockSpec((tq, DIFF_W), lambda b, i: (b * nq + i, 0)),
        out_shape=jax.ShapeDtypeStruct((bsz * seq, DIFF_W), BF16),
        scratch_shapes=[pltpu.VMEM((nslab * tq, 1), F32), pltpu.VMEM((nslab * tq, 1), F32),
                        pltpu.VMEM((nslab * tq, DIFF_W), F32)],
        compiler_params=_cparams(("arbitrary", "arbitrary")),
        name="diff_attn",
    )(qd, kd, vd, lamv, g4)


def _mla_attn_body(tq, q_ref, k_ref, v_ref, o_ref, m_ref, l_ref, acc1_ref, acc2_ref):
    i = pl.program_id(1)
    tk = tq
    q = q_ref[...]
    acc1_ref[...] = jnp.zeros(acc1_ref.shape, F32)
    acc2_ref[...] = jnp.zeros(acc2_ref.shape, F32)
    mask = _chunk_mask(C_HEADS * tq, tq, tk)
    g1 = 4

    def score(j):
        kt = k_ref[pl.ds(pl.multiple_of(j * tk, tk), tk), :]
        return jnp.concatenate([_nt(q[:, MLA_SLAB * hd:MLA_SLAB * (hd + 1)], kt[:, MLA_SLAB * hd:MLA_SLAB * (hd + 1)])
                                for hd in range(C_HEADS)], axis=0)

    def pv(j, alpha, pb):
        vt = v_ref[pl.ds(pl.multiple_of(j * tk, tk), tk), :]
        acc1_ref[...] = alpha[0:g1 * tq] * acc1_ref[...] + _nn(pb[0:g1 * tq], vt[:, 0:256])
        acc2_ref[...] = alpha[g1 * tq:] * acc2_ref[...] + _nn(pb[g1 * tq:], vt[:, 256:MLA_V_W])

    _flash_loop(i, tk, score, pv, mask, m_ref, l_ref)

    linv = 1.0 / l_ref[...]
    o1 = acc1_ref[...] * linv[0:g1 * tq]
    o2 = acc2_ref[...] * linv[g1 * tq:]
    lane1 = lax.broadcasted_iota(jnp.int32, (1, 256), 1)
    lane2 = lax.broadcasted_iota(jnp.int32, (1, 128), 1)
    out1 = jnp.zeros((tq, 256), F32)
    for hd in range(g1):
        out1 = out1 + jnp.where((lane1 >= C_V_DIM * hd) & (lane1 < C_V_DIM * (hd + 1)), o1[hd * tq:(hd + 1) * tq], 0.0)
    out2 = jnp.zeros((tq, 128), F32)
    for hd in range(C_HEADS - g1):
        out2 = out2 + jnp.where((lane2 >= C_V_DIM * hd) & (lane2 < C_V_DIM * (hd + 1)), o2[hd * tq:(hd + 1) * tq], 0.0)
    o_ref[...] = jnp.concatenate([out1, out2], axis=1).astype(BF16)


def _mla_attn(qc, kc, vc, bsz, seq):
    tq = min(256, seq)
    nq = seq // tq
    return pl.pallas_call(
        functools.partial(_mla_attn_body, tq),
        grid=(bsz, nq),
        in_specs=[pl.BlockSpec((tq, MLA_QK_W), lambda b, i: (b * nq + i, 0)),
                  pl.BlockSpec((seq, MLA_QK_W), lambda b, i: (b, 0)),
                  pl.BlockSpec((seq, MLA_V_W), lambda b, i: (b, 0))],
        out_specs=pl.BlockSpec((tq, MLA_V_W), lambda b, i: (b * nq + i, 0)),
        out_shape=jax.ShapeDtypeStruct((bsz * seq, MLA_V_W), BF16),
        scratch_shapes=[pltpu.VMEM((C_HEADS * tq, 1), F32), pltpu.VMEM((C_HEADS * tq, 1), F32),
                        pltpu.VMEM((4 * tq, 256), F32), pltpu.VMEM((2 * tq, 128), F32)],
        compiler_params=_cparams(("arbitrary", "arbitrary")),
        name="mla_attn",
    )(qc, kc, vc)


def _mlstm_body(nc, q_ref, k_ref, v_ref, og_ref, gates_ref, gbias_ref, gout_ref, o_ref, ct_ref, m_ref):
    ct_ref[...] = jnp.zeros(ct_ref.shape, F32)
    m_ref[...] = jnp.zeros(m_ref.shape, F32)
    lane = lax.broadcasted_iota(jnp.int32, (1, LANES), 1)
    ti = lax.broadcasted_iota(jnp.int32, (CHUNK, CHUNK), 0)
    si = lax.broadcasted_iota(jnp.int32, (CHUNK, CHUNK), 1)
    tril = si <= ti
    eye = si == ti
    eye128 = (lax.broadcasted_iota(jnp.int32, (LANES, LANES), 0)
              == lax.broadcasted_iota(jnp.int32, (LANES, LANES), 1)).astype(BF16)
    gbias = gbias_ref[...]
    gout = gout_ref[...]

    def chunk(c, carry):
        rows = pl.ds(pl.multiple_of(c * CHUNK, CHUNK), CHUNK)
        gch = gates_ref[0, c] + gbias
        ig = gch[0:8]
        fg = gch[8:16]
        lf = jnp.minimum(fg, 0.0) - jnp.log(1.0 + jnp.exp(-jnp.abs(fg)))
        b = jnp.where(lane < CHUNK, lf, 0.0)
        for sh in (1, 2, 4, 8, 16, 32):
            b = b + jnp.where(lane >= sh, pltpu.roll(b, sh, 1), 0.0)
        a = ig - b
        a_last = jnp.max(jnp.where(lane < CHUNK, a, NEG_BIG), axis=1, keepdims=True)
        b_last = jnp.sum(jnp.where(lane == CHUNK - 1, b, 0.0), axis=1, keepdims=True)

        for pr in range(B_HEADS // 2):
            cols = slice(LANES * pr, LANES * (pr + 1))
            qw = q_ref[rows, cols]
            kw = k_ref[rows, cols]
            vw = v_ref[rows, cols]
            kt = _nt(eye128, kw)
            ctp = ct_ref[pr]
            ctb = ctp.astype(BF16)
            outs = []
            for hh in range(2):
                hd = 2 * pr + hh
                half = (lane < CHUNK) if hh == 0 else (lane >= CHUNK)
                onecol = (lane == CHUNK) if hh == 0 else (lane == 0)
                qmk = jnp.where(half, qw, jnp.zeros_like(qw))
                vaug = jnp.where(half, vw, jnp.where(onecol, 1.0, 0.0).astype(BF16))
                a_h = a[hd:hd + 1, 0:CHUNK]
                b_h = b[hd:hd + 1, 0:CHUNK]
                a_col = jnp.max(jnp.where(tril, a_h, NEG_BIG), axis=1, keepdims=True)
                b_col = jnp.sum(jnp.where(eye, b_h, 0.0), axis=1, keepdims=True)
                w0 = jnp.where(tril, jnp.exp(a_h - a_col), 0.0)
                sc0 = (_nt(qmk, kw) * w0).astype(BF16)
                x0 = _nn(sc0, vaug)
                xi = _nn(qmk, ctb)
                m = m_ref[hd:hd + 1, 0:1]
                m_col = jnp.maximum(a_col, m)
                xs = jnp.exp(a_col - m_col) * x0 + jnp.exp(m - m_col) * xi
                den = jnp.sum(jnp.where(onecol, xs, 0.0), axis=1, keepdims=True)
                hv = xs / jnp.maximum(jnp.abs(den), jnp.exp(-b_col - m_col))
                ms = jnp.sum(jnp.where(half, hv * hv, 0.0), axis=1, keepdims=True) * (1.0 / B_DIM)
                outs.append(hv * lax.rsqrt(ms + EPS))
                al = a_last[hd:hd + 1]
                m_l = jnp.maximum(al, m)
                kwt = (kt[CHUNK * hh:CHUNK * (hh + 1), :] * jnp.exp(a_h - al)).astype(BF16)
                dct = _nn(kwt, vaug)
                ct_ref[pr, CHUNK * hh:CHUNK * (hh + 1), :] = (jnp.exp(m - m_l) * ctp[CHUNK * hh:CHUNK * (hh + 1)]
                                                              + jnp.exp(al - m_l) * dct)
                m_ref[hd:hd + 1, :] = jnp.broadcast_to(b_last[hd:hd + 1] + m_l, (1, LANES))
            hn = jnp.where(lane < CHUNK, outs[0], outs[1])
            og = og_ref[rows, cols].astype(F32)
            o_ref[rows, cols] = (hn * gout[:, cols] * _sigmoid(og)).astype(BF16)
        return carry

    lax.fori_loop(0, nc, chunk, 0)


def _mlstm(qm, km, vm, om, gates, gbias, gout, bsz, seq):
    nc = seq // CHUNK
    blk = pl.BlockSpec((seq, MLSTM_W), lambda b: (b, 0))
    return pl.pallas_call(
        functools.partial(_mlstm_body, nc),
        grid=(bsz,),
        in_specs=[blk, blk, blk, blk,
                  pl.BlockSpec((1, nc, 16, LANES), lambda b: (b, 0, 0, 0)),
                  pl.BlockSpec(gbias.shape, lambda b: (0, 0)),
                  pl.BlockSpec(gout.shape, lambda b: (0, 0))],
        out_specs=blk,
        out_shape=jax.ShapeDtypeStruct((bsz * seq, MLSTM_W), BF16),
        scratch_shapes=[pltpu.VMEM((B_HEADS // 2, LANES, LANES), F32), pltpu.VMEM((8, LANES), F32)],
        compiler_params=_cparams(("arbitrary",)),
        name="mlstm",
    )(qm, km, vm, om, gates, gbias, gout)


def _mix_out_body(with_router, tm,
                  oa_ref, ob_ref, oc_ref, wout_ref, x_ref, gate_ref, g1_ref, shift_ref, scale_ref, g2_ref, *rest):
    if with_router:
        wr_ref, br_ref, xo_ref, h_ref, ri_ref, rg_ref, cnt_ref, run_ref = rest
    else:
        xo_ref, h_ref = rest
    na = A_HEADS * A_V_DIM
    y = (_nn(oa_ref[...], wout_ref[0:na, :]) + _nn(ob_ref[...], wout_ref[na:na + MLSTM_W, :])
         + _nn(oc_ref[...], wout_ref[na + MLSTM_W:, :]))
    xn = x_ref[...] + gate_ref[0] * (_rms(y, D_MODEL) * g1_ref[...])
    xo_ref[...] = xn
    h = _rms(xn, D_MODEL) * g2_ref[...]
    h = h * (1.0 + scale_ref[0]) + shift_ref[0]
    h_ref[...] = h.astype(BF16)
    if with_router:
        @pl.when(pl.program_id(0) == 0)
        def _():
            run_ref[...] = jnp.zeros(run_ref.shape, F32)

        lane = lax.broadcasted_iota(jnp.int32, (1, LANES), 1)
        logits = jnp.dot(h, wr_ref[...], precision=HIGHEST, preferred_element_type=F32) + br_ref[...]
        lg = jnp.where(lane < N_EXPERTS, logits, NEG_BIG)
        v1 = jnp.max(lg, axis=1, keepdims=True)
        i1 = jnp.min(jnp.where(lg == v1, lane, LANES), axis=1, keepdims=True)
        lg2 = jnp.where(lane == i1, NEG_BIG, lg)
        v2 = jnp.max(lg2, axis=1, keepdims=True)
        i2 = jnp.min(jnp.where(lg2 == v2, lane, LANES), axis=1, keepdims=True)
        gt1 = 1.0 / (1.0 + jnp.exp(v2 - v1))
        oh1 = lane == i1
        oh2 = lane == i2
        cnt = jnp.where(oh1 | oh2, 1.0, 0.0)
        ri_ = lax.broadcasted_iota(jnp.int32, (tm, tm), 0)
        ci_ = lax.broadcasted_iota(jnp.int32, (tm, tm), 1)
        lower = jnp.where(ci_ < ri_, 1.0, 0.0).astype(BF16)
        before = _nn(lower, cnt.astype(BF16)) + run_ref[0:1, :]
        r1 = jnp.sum(jnp.where(oh1, before, 0.0), axis=1, keepdims=True)
        r2 = jnp.sum(jnp.where(oh2, before, 0.0), axis=1, keepdims=True)
        run_ref[...] = run_ref[...] + jnp.sum(cnt, axis=0, keepdims=True)
        ri_ref[...] = jnp.where(lane == 0, i1, jnp.where(lane == 1, i2, jnp.where(
            lane == 2, r1.astype(jnp.int32), jnp.where(lane == 3, r2.astype(jnp.int32), 0))))
        rg_ref[...] = jnp.where(lane == 0, gt1, jnp.where(lane == 1, 1.0 - gt1, 0.0))
        cnt_ref[...] = run_ref[...]


def _mix_out(oa, ob, oc, wout, x2, gate, g1, shift, scale, g2, router, bsz, seq):
    n = bsz * seq
    d = D_MODEL
    tm = min(512, seq)
    tps = seq // tm
    row = lambda w: pl.BlockSpec((tm, w), lambda r: (r, 0))
    full = lambda a: pl.BlockSpec(a.shape, lambda r: (0,) * a.ndim)
    mod = pl.BlockSpec((1, 1, d), lambda r: (r // tps, 0, 0))
    in_specs = [row(oa.shape[1]), row(ob.shape[1]), row(oc.shape[1]), full(wout), row(d), mod, full(g1), mod, mod, full(g2)]
    args = [oa, ob, oc, wout, x2, gate, g1, shift, scale, g2]
    out_shape = [jax.ShapeDtypeStruct((n, d), F32), jax.ShapeDtypeStruct((n, d), BF16)]
    out_specs = [row(d), row(d)]
    scratch = []
    if router is not None:
        wr, br = router
        in_specs += [full(wr), full(br)]
        args += [wr, br]
        out_shape += [jax.ShapeDtypeStruct((n, LANES), jnp.int32), jax.ShapeDtypeStruct((n, LANES), F32),
                      jax.ShapeDtypeStruct((8, LANES), F32)]
        out_specs += [row(LANES), row(LANES), pl.BlockSpec((8, LANES), lambda r: (0, 0))]
        scratch = [pltpu.VMEM((8, LANES), F32)]
    return pl.pallas_call(
        functools.partial(_mix_out_body, router is not None, tm),
        grid=(n // tm,),
        in_specs=in_specs,
        out_specs=out_specs,
        out_shape=out_shape,
        scratch_shapes=scratch,
        compiler_params=_cparams(("arbitrary",)),
        name="mix_out_router" if router is not None else "mix_out",
    )(*args)


def _ffn_dense_body(h_ref, wg_ref, wu_ref, wd_ref, x_ref, gate_ref, g_ref, o_ref, acc_ref):
    j = pl.program_id(1)

    @pl.when(j == 0)
    def _():
        acc_ref[...] = jnp.zeros(acc_ref.shape, F32)

    h = h_ref[...]
    a = _nn(h, wg_ref[...])
    u = _nn(h, wu_ref[...])
    acc_ref[...] += _nn((a * _sigmoid(a) * u).astype(BF16), wd_ref[...])

    @pl.when(j == pl.num_programs(1) - 1)
    def _():
        o_ref[...] = x_ref[...] + gate_ref[0] * (_rms(acc_ref[...], D_MODEL) * g_ref[...])


def _ffn_dense(h, wg, wu, wd, x2, gate, g, bsz, seq):
    n = bsz * seq
    d = D_MODEL
    tm = min(1024, seq)
    tps = seq // tm
    tf = 512
    return pl.pallas_call(
        _ffn_dense_body,
        grid=(n // tm, D_FF // tf),
        in_specs=[pl.BlockSpec((tm, d), lambda i, j: (i, 0)),
                  pl.BlockSpec((d, tf), lambda i, j: (0, j)),
                  pl.BlockSpec((d, tf), lambda i, j: (0, j)),
                  pl.BlockSpec((tf, d), lambda i, j: (j, 0)),
                  pl.BlockSpec((tm, d), lambda i, j: (i, 0)),
                  pl.BlockSpec((1, 1, d), lambda i, j: (i // tps, 0, 0)),
                  pl.BlockSpec(g.shape, lambda i, j: (0, 0))],
        out_specs=pl.BlockSpec((tm, d), lambda i, j: (i, 0)),
        out_shape=jax.ShapeDtypeStruct((n, d), F32),
        scratch_shapes=[pltpu.VMEM((tm, d), F32)],
        compiler_params=_cparams(("arbitrary", "arbitrary")),
        name="ffn_dense",
    )(h, wg, wu, wd, x2, gate, g)


def _ffn_expert_body(blk_e_ref, nvalid_ref, x_ref, wg_ref, wu_ref, wd_ref, o_ref, acc_ref):
    i = pl.program_id(0)
    j = pl.program_id(1)
    valid = i < nvalid_ref[0]

    @pl.when(j == 0)
    def _():
        acc_ref[...] = jnp.zeros(acc_ref.shape, F32)

    @pl.when(valid)
    def _():
        xb = x_ref[...]
        a = _nn(xb, wg_ref[0])
        u = _nn(xb, wu_ref[0])
        acc_ref[...] += _nn((a * _sigmoid(a) * u).astype(BF16), wd_ref[0])

    @pl.when(j == pl.num_programs(1) - 1)
    def _():
        o_ref[...] = acc_ref[...]


def _ffn_expert(xb, wg, wu, wd, blk_e, nvalid, tmb):
    p, d = xb.shape
    nb = p // tmb
    tf = 512
    nf = D_FF // tf

    def jeff(i, j, nv):
        return jnp.where(i < nv[0], j, nf - 1)

    grid_spec = pltpu.PrefetchScalarGridSpec(
        num_scalar_prefetch=2,
        grid=(nb, nf),
        in_specs=[pl.BlockSpec((tmb, d), lambda i, j, be, nv: (i, 0)),
                  pl.BlockSpec((1, d, tf), lambda i, j, be, nv: (be[i], 0, jeff(i, j, nv))),
                  pl.BlockSpec((1, d, tf), lambda i, j, be, nv: (be[i], 0, jeff(i, j, nv))),
                  pl.BlockSpec((1, tf, d), lambda i, j, be, nv: (be[i], jeff(i, j, nv), 0))],
        out_specs=pl.BlockSpec((tmb, d), lambda i, j, be, nv: (i, 0)),
        scratch_shapes=[pltpu.VMEM((tmb, d), F32)],
    )
    return pl.pallas_call(
        _ffn_expert_body,
        grid_spec=grid_spec,
        out_shape=jax.ShapeDtypeStruct((p, d), F32),
        compiler_params=_cparams(("arbitrary", "arbitrary")),
        name="ffn_expert",
    )(blk_e, nvalid, xb, wg, wu, wd)


def _moe_combine_body(y1_ref, y2_ref, rg_ref, x_ref, gate_ref, g_ref, o_ref):
    rg = rg_ref[...]
    y = rg[:, 0:1] * y1_ref[...] + rg[:, 1:2] * y2_ref[...]
    o_ref[...] = x_ref[...] + gate_ref[0] * (_rms(y, D_MODEL) * g_ref[...])


def _moe_combine(y1, y2, rg, x2, gate, g, bsz, seq):
    n = bsz * seq
    d = D_MODEL
    tm = min(512, seq)
    tps = seq // tm
    row = lambda w: pl.BlockSpec((tm, w), lambda r: (r, 0))
    return pl.pallas_call(
        _moe_combine_body,
        grid=(n // tm,),
        in_specs=[row(d), row(d), row(LANES), row(d), pl.BlockSpec((1, 1, d), lambda r: (r // tps, 0, 0)),
                  pl.BlockSpec(g.shape, lambda r: (0, 0))],
        out_specs=row(d),
        out_shape=jax.ShapeDtypeStruct((n, d), F32),
        compiler_params=_cparams(("arbitrary",)),
        name="moe_combine",
    )(y1, y2, rg, x2, gate, g)


def _moe_ffn(h, ri, rg, counts, wg, wu, wd, x2, gate, g, bsz, seq):
    n = bsz * seq
    tmb = min(1024, n // 8)
    cnt = counts[0, :N_EXPERTS].astype(jnp.int32)
    padded = ((cnt + tmb - 1) // tmb) * tmb
    pend = jnp.cumsum(padded)
    pstart = pend - padded
    dest1 = pstart[ri[:, 0]] + ri[:, 2]
    dest2 = pstart[ri[:, 1]] + ri[:, 3]
    nb = (2 * n) // tmb + N_EXPERTS
    p = nb * tmb
    tok = jnp.arange(n, dtype=jnp.int32)
    buf_tok = jnp.zeros((p,), jnp.int32).at[dest1].set(tok).at[dest2].set(tok)
    nvalid = (pend[-1] // tmb).astype(jnp.int32).reshape(1)
    blk = jnp.minimum(jnp.arange(nb, dtype=jnp.int32), nvalid[0] - 1) * tmb
    blk_e = jnp.minimum(jnp.searchsorted(pend, blk, side='right'), N_EXPERTS - 1).astype(jnp.int32)
    xb = jnp.take(h, buf_tok, axis=0)
    yb = _ffn_expert(xb, wg, wu, wd, blk_e, nvalid, tmb)
    y1 = jnp.take(yb, dest1, axis=0)
    y2 = jnp.take(yb, dest2, axis=0)
    return _moe_combine(y1, y2, rg, x2, gate, g, bsz, seq)


def _rot_cols(w):
    half = C_ROPE // 2
    return jnp.concatenate([-w[:, half:], w[:, :half]], axis=1)


def _prep_w_in(w_in):
    d = w_in.shape[0]
    z = lambda k: jnp.zeros((d, k), w_in.dtype)
    gates = w_in[:, 2304:2316]
    cq = w_in[:, 2316:2508]
    ckv = w_in[:, 2508:2636]
    kr = w_in[:, 2636:2668]
    cols = [w_in[:, :2304], cq, z(64), ckv,
            gates, z(C_NOPE - 12), kr, z(LANES - C_NOPE - C_ROPE),
            z(C_NOPE), _rot_cols(kr), z(LANES - C_NOPE - C_ROPE)]
    out = jnp.concatenate(cols, axis=1).astype(BF16)
    assert out.shape[1] == IN_COLS
    return out


def _prep_w_uq(w_uq):
    w = w_uq.reshape(C_Q_LORA, C_HEADS, C_NOPE + C_ROPE)
    nope, rp = w[..., :C_NOPE], w[..., C_NOPE:]
    pad = jnp.zeros((C_Q_LORA, C_HEADS, MLA_SLAB - C_NOPE - C_ROPE), w.dtype)
    rot = jnp.concatenate([-rp[..., C_ROPE // 2:], rp[..., :C_ROPE // 2]], axis=-1)
    wq = jnp.concatenate([nope, rp, pad], axis=-1).reshape(C_Q_LORA, MLA_QK_W)
    wqr = jnp.concatenate([jnp.zeros_like(nope), rot, pad], axis=-1).reshape(C_Q_LORA, MLA_QK_W)
    rowpad = jnp.zeros((256 - C_Q_LORA, MLA_QK_W), w.dtype)
    return (jnp.concatenate([wq, rowpad], axis=0).astype(BF16), jnp.concatenate([wqr, rowpad], axis=0).astype(BF16))


def _prep_w_ukv(w_ukv):
    w = w_ukv.reshape(C_KV_LORA, C_HEADS, C_NOPE + C_V_DIM)
    kn = jnp.concatenate([w[..., :C_NOPE], jnp.zeros((C_KV_LORA, C_HEADS, MLA_SLAB - C_NOPE), w.dtype)], axis=-1)
    v = w[..., C_NOPE:]
    return jnp.concatenate([kn.reshape(C_KV_LORA, MLA_QK_W), v.reshape(C_KV_LORA, MLA_V_W)], axis=1).astype(BF16)


def _rope_tables(positions):
    freqs = ROPE_THETA ** (-jnp.arange(0, C_ROPE, 2, dtype=F32) / C_ROPE)
    ang = positions.astype(F32).reshape(-1, 1) * freqs
    n = ang.shape[0]
    cos, sin = jnp.cos(ang), jnp.sin(ang)
    ones = lambda k: jnp.ones((n, k), F32)
    zeros = lambda k: jnp.zeros((n, k), F32)
    tail = MLA_SLAB - C_NOPE - C_ROPE
    return (jnp.concatenate([ones(C_NOPE), cos, cos, ones(tail)], axis=1),
            jnp.concatenate([zeros(C_NOPE), sin, sin, zeros(tail)], axis=1))


def kernel(x, c, positions, w_ada, b_ada, g_norm, w_in, w_out, lam_q1, lam_k1, lam_q2, lam_k2, g_diff, w_conv, b_conv, b_igate, b_fgate, g_mlstm, g_q_lat, g_kv_lat, w_uq, w_ukv, w_gate_d, w_up_d, w_down_d, w_router, b_router, w_gate_e, w_up_e, w_down_e):
    bsz, seq, d = x.shape
    n = bsz * seq
    nc = seq // CHUNK
    depth = w_in.shape[0]
    mods = _ada_all(c, w_ada, b_ada)
    cos_t, sin_t = _rope_tables(positions)
    x2 = x.reshape(n, d)
    row1 = lambda v: v.reshape(1, -1).astype(F32)

    for l in range(depth):
        mod = lambda k, j: mods[2 * l + k, j].reshape(bsz, 1, d)
        wq, wqr = _prep_w_uq(w_uq[l])
        gq = jnp.concatenate([g_q_lat[l], jnp.zeros((256 - C_Q_LORA,), F32)]).reshape(1, 256)
        (qd, kd, vd, qm, km, vm, om, gt, qc, kc, vc) = _mix_in(
            x2, mod(0, 0), mod(0, 1), row1(g_norm[l, 0]), _prep_w_in(w_in[l]), w_conv[l], row1(b_conv[l]),
            gq, row1(g_kv_lat[l]), wq, wqr, _prep_w_ukv(w_ukv[l]), cos_t, sin_t, bsz, seq)

        lam_init = 0.8 - 0.6 * math.exp(-0.3 * l)
        lamv = jnp.zeros((8, LANES), F32).at[0:4, 0:A_QK_DIM].set(jnp.stack([lam_q1[l], lam_k1[l], lam_q2[l], lam_k2[l]]))
        out_a = _diff_attn(qd, kd, vd, lamv, row1(jnp.tile(g_diff[l], A_HEADS)), lam_init, bsz, seq)

        z2 = jnp.zeros((bsz, 2, seq), F32)
        g16 = jnp.concatenate([gt[:, 0:6], z2, gt[:, 6:12], z2], axis=1)
        gates = jnp.pad(g16.reshape(bsz, 16, nc, CHUNK).transpose(0, 2, 1, 3), ((0, 0), (0, 0), (0, 0), (0, LANES - CHUNK)))
        zb = jnp.zeros((2,), F32)
        gbias = jnp.broadcast_to(jnp.concatenate([b_igate[l], zb, b_fgate[l], zb])[:, None], (16, LANES)).astype(F32)
        out_b = _mlstm(qm, km, vm, om, gates, gbias, row1(g_mlstm[l]), bsz, seq)

        out_c = _mla_attn(qc, kc, vc, bsz, seq)

        moe = (l % 2 == 1)
        router = None
        if moe:
            wr = jnp.pad(w_router[l // 2], ((0, 0), (0, LANES - N_EXPERTS)))
            br = jnp.pad(b_router[l // 2], (0, LANES - N_EXPERTS)).reshape(1, LANES)
            router = (wr, br)
        res = _mix_out(out_a, out_b, out_c, w_out[l].astype(BF16), x2, mod(0, 2), row1(g_norm[l, 1]),
                       mod(1, 0), mod(1, 1), row1(g_norm[l, 2]), router, bsz, seq)
        if moe:
            x2, h2, ri, rg, counts = res
            x2 = _moe_ffn(h2, ri, rg, counts, w_gate_e[l // 2].astype(BF16), w_up_e[l // 2].astype(BF16),
                          w_down_e[l // 2].astype(BF16), x2, mod(1, 2), row1(g_norm[l, 3]), bsz, seq)
        else:
            x2, h2 = res
            x2 = _ffn_dense(h2, w_gate_d[l // 2].astype(BF16), w_up_d[l // 2].astype(BF16),
                            w_down_d[l // 2].astype(BF16), x2, mod(1, 2), row1(g_norm[l, 3]), bsz, seq)
    return x2.reshape(bsz, seq, d)
```

```python
import functools
import math

import jax
import jax.numpy as jnp
import numpy as np
from jax import lax
from jax.experimental import pallas as pl
from jax.experimental.pallas import tpu as pltpu

F32 = jnp.float32
BF16 = jnp.bfloat16
HIGHEST = lax.Precision.HIGHEST

D_MODEL = 1024
DEPTH = 2
CHUNK = 64
EPS = 1e-6
A_HEADS, A_QK_DIM, A_V_DIM = 4, 32, 64
B_HEADS, B_DIM, CONV_WIDTH = 6, 64, 4
C_HEADS, C_Q_LORA, C_KV_LORA, C_NOPE, C_ROPE, C_V_DIM = 6, 192, 128, 64, 32, 64
ROPE_THETA = 10000.0
D_FF = 3584
N_EXPERTS = 8
LOG2E = 1.4426950408889634
NEG_BIG = -1e30

LANES = 128
DIFF_W = 2 * A_HEADS * A_QK_DIM
MLSTM_W = B_HEADS * B_DIM
MLA_SLAB = 128
MLA_QK_W = C_HEADS * MLA_SLAB
MLA_V_W = C_HEADS * C_V_DIM

COL_DIFF = 0
COL_MLSTM = 768
COL_CQ = 2304
COL_CKV = 2560
COL_X = 2688
COL_Y = 2816
IN_COLS = 2944

VMEM_LIMIT = 56 * 1024 * 1024


def _cparams(sem):
    return pltpu.CompilerParams(dimension_semantics=sem, vmem_limit_bytes=VMEM_LIMIT)


def _nt(a, b):
    return lax.dot_general(a, b, (((1,), (1,)), ((), ())), preferred_element_type=F32)


def _nn(a, b):
    return jnp.dot(a, b, preferred_element_type=F32)


def _sigmoid(v):
    return 1.0 / (1.0 + jnp.exp(-v))


def _rms(v, n):
    return v * lax.rsqrt(jnp.sum(v * v, axis=-1, keepdims=True) * (1.0 / n) + EPS)


def _ada_body(c_ref, w_ref, b_ref, o_ref):
    c = c_ref[...]
    sc = c * _sigmoid(c)
    o_ref[0, 0] = jnp.dot(sc, w_ref[0], precision=HIGHEST, preferred_element_type=F32) + b_ref[0]


def _ada_all(c, w_ada, b_ada):
    bsz, d = c.shape
    n = w_ada.shape[0] * w_ada.shape[1]
    w = w_ada.reshape(n, d, 3 * d)
    b = b_ada.reshape(n, 1, 3 * d)
    return pl.pallas_call(
        _ada_body,
        grid=(n, 3),
        in_specs=[pl.BlockSpec((bsz, d), lambda i, j: (0, 0)),
                  pl.BlockSpec((1, d, d), lambda i, j: (i, 0, j)),
                  pl.BlockSpec((1, 1, d), lambda i, j: (i, 0, j))],
        out_specs=pl.BlockSpec((1, 1, bsz, d), lambda i, j: (i, j, 0, 0)),
        out_shape=jax.ShapeDtypeStruct((n, 3, bsz, d), F32),
        compiler_params=_cparams(("arbitrary", "arbitrary")),
        name="ada_mod",
    )(c, w, b)


def _mix_in_body(tiles_per_seq, tm,
                 x_ref, shift_ref, scale_ref, g_ref, win_ref, wconv_ref, bconv_ref, gq_ref, gkv_ref,
                 wq_ref, wqr_ref, wkv_ref, cos_ref, sin_ref,
                 qd_ref, kd_ref, vd_ref, qm_ref, km_ref, vm_ref, om_ref, gt_ref, qc_ref, kc_ref, vc_ref,
                 carry_ref):
    r = pl.program_id(0)
    x = x_ref[...]
    h = _rms(x, D_MODEL) * g_ref[...]
    h = h * (1.0 + scale_ref[0]) + shift_ref[0]
    p = _nn(h.astype(BF16), win_ref[...])

    qd_ref[...] = (p[:, 0:256] * (A_QK_DIM ** -0.5 * LOG2E)).astype(BF16)
    kd_ref[...] = p[:, 256:512].astype(BF16)
    vd_ref[...] = p[:, 512:768].astype(BF16)

    @pl.when(r % tiles_per_seq == 0)
    def _():
        carry_ref[0:8, :] = jnp.zeros((8, 2 * MLSTM_W), F32)

    carry_ref[8:8 + tm, :] = p[:, COL_MLSTM:COL_MLSTM + 2 * MLSTM_W]
    w = wconv_ref[...]
    conv = bconv_ref[...] + carry_ref[8:8 + tm, :] * w[3:4, :]
    for j in range(CONV_WIDTH - 1):
        conv = conv + carry_ref[5 + j:5 + j + tm, :] * w[j:j + 1, :]
    carry_ref[0:8, :] = carry_ref[tm:tm + 8, :]
    qk = conv * _sigmoid(conv)
    qm_ref[...] = qk[:, 0:MLSTM_W].astype(BF16)
    km_ref[...] = (qk[:, MLSTM_W:2 * MLSTM_W] * (B_DIM ** -0.5)).astype(BF16)
    vm_ref[...] = p[:, COL_MLSTM + 2 * MLSTM_W:COL_MLSTM + 3 * MLSTM_W].astype(BF16)
    om_ref[...] = p[:, COL_MLSTM + 3 * MLSTM_W:COL_MLSTM + 4 * MLSTM_W].astype(BF16)

    xg = p[:, COL_X:COL_X + LANES]
    gt_ref[0] = xg.T[0:16, :]

    cq = p[:, COL_CQ:COL_CQ + 256]
    cqn = (cq * lax.rsqrt(jnp.sum(cq * cq, axis=-1, keepdims=True) * (1.0 / C_Q_LORA) + EPS) * gq_ref[...]).astype(BF16)
    ckv = p[:, COL_CKV:COL_CKV + C_KV_LORA]
    ckvn = (_rms(ckv, C_KV_LORA) * gkv_ref[...]).astype(BF16)
    cos_t = cos_ref[...]
    sin_t = sin_ref[...]
    cos6 = jnp.concatenate([cos_t] * C_HEADS, axis=1)
    sin6 = jnp.concatenate([sin_t] * C_HEADS, axis=1)
    qn = _nn(cqn, wq_ref[...])
    qr = _nn(cqn, wqr_ref[...])
    qc_ref[...] = ((qn * cos6 + qr * sin6) * ((C_NOPE + C_ROPE) ** -0.5 * LOG2E)).astype(BF16)
    kvp = _nn(ckvn, wkv_ref[...])
    lane = lax.broadcasted_iota(jnp.int32, (1, LANES), 1)
    rope_lanes = (lane >= C_NOPE) & (lane < C_NOPE + C_ROPE)
    yg = p[:, COL_Y:COL_Y + LANES]
    krope = jnp.where(rope_lanes, xg * cos_t + yg * sin_t, 0.0)
    kc_ref[...] = (kvp[:, 0:MLA_QK_W] + jnp.concatenate([krope] * C_HEADS, axis=1)).astype(BF16)
    vc_ref[...] = kvp[:, MLA_QK_W:MLA_QK_W + MLA_V_W].astype(BF16)


def _mix_in(x2, shift, scale, g, win, wconv, bconv, gq, gkv, wq, wqr, wkv, cos_t, sin_t, bsz, seq):
    n = bsz * seq
    tm = min(512, seq)
    tps = seq // tm
    d = D_MODEL
    row = lambda w: pl.BlockSpec((tm, w), lambda r: (r, 0))
    full = lambda a: pl.BlockSpec(a.shape, lambda r: (0,) * a.ndim)
    mod = pl.BlockSpec((1, 1, d), lambda r: (r // tps, 0, 0))
    out_widths = [DIFF_W, DIFF_W, DIFF_W, MLSTM_W, MLSTM_W, MLSTM_W, MLSTM_W]
    out_shape = [jax.ShapeDtypeStruct((n, w), BF16) for w in out_widths]
    out_specs = [row(w) for w in out_widths]
    out_shape.append(jax.ShapeDtypeStruct((bsz, 16, seq), F32))
    out_specs.append(pl.BlockSpec((1, 16, tm), lambda r: (r // tps, 0, r % tps)))
    for w in (MLA_QK_W, MLA_QK_W, MLA_V_W):
        out_shape.append(jax.ShapeDtypeStruct((n, w), BF16))
        out_specs.append(row(w))
    return pl.pallas_call(
        functools.partial(_mix_in_body, tps, tm),
        grid=(n // tm,),
        in_specs=[row(d), mod, mod, full(g), full(win), full(wconv), full(bconv), full(gq), full(gkv),
                  full(wq), full(wqr), full(wkv), row(LANES), row(LANES)],
        out_specs=out_specs,
        out_shape=out_shape,
        scratch_shapes=[pltpu.VMEM((tm + 8, 2 * MLSTM_W), F32)],
        compiler_params=_cparams(("arbitrary",)),
        name="mix_in",
    )(x2, shift, scale, g, win, wconv, bconv, gq, gkv, wq, wqr, wkv, cos_t, sin_t)


def _chunk_mask(rows, tq, tk):
    ri = lax.broadcasted_iota(jnp.int32, (rows, tk), 0)
    ci = lax.broadcasted_iota(jnp.int32, (rows, tk), 1)
    return (ci >> 6) <= ((ri & (tq - 1)) >> 6)


def _two_pass_softmax(i, nq, tk, score_fn, pv_fn, mask, s_ref, p_ref, mx_ref, l_ref):
    mx_ref[...] = jnp.full(mx_ref.shape, NEG_BIG, F32)

    def pass1(j, masked):
        s = score_fn(j)
        if masked:
            s = jnp.where(mask, s, NEG_BIG)
        s_ref[j] = s
        mx_ref[...] = jnp.maximum(mx_ref[...], jnp.maximum(s[:, 0:LANES], s[:, LANES:2 * LANES]))

    def body1(j, c):
        pass1(j, False)
        return c

    lax.fori_loop(0, i, body1, 0)
    pass1(i, True)
    m = jnp.max(mx_ref[...], axis=1, keepdims=True)
    l_ref[...] = jnp.zeros(l_ref.shape, F32)

    def body2(j, c):
        s = s_ref[j]
        p0 = jnp.exp2(s[:, 0:LANES] - m)
        p1 = jnp.exp2(s[:, LANES:2 * LANES] - m)
        l_ref[...] += p0 + p1
        p_ref[j] = jnp.concatenate([p0, p1], axis=1).astype(BF16)
        return c

    lax.fori_loop(0, i + 1, body2, 0)
    for n in range(1, nq + 1):
        @pl.when(i == n - 1)
        def _():
            pv_fn(n)
    return jnp.sum(l_ref[...], axis=1, keepdims=True)


def _diff_attn_body(tq, nq, lam_init, q_ref, k_ref, v_ref, lam_ref, g_ref, o_ref,
                    qs_ref, s_ref, p_ref, mx_ref, l_ref, acc_ref):
    i = pl.program_id(1)
    tk = tq
    nslab = 2 * A_HEADS
    q = q_ref[...]
    lane = lax.broadcasted_iota(jnp.int32, (1, DIFF_W), 1)
    zero = jnp.zeros_like(q)
    for s in range(nslab):
        qs_ref[s * tq:(s + 1) * tq, :] = jnp.where((lane >= A_QK_DIM * s) & (lane < A_QK_DIM * (s + 1)), q, zero)
    mask = _chunk_mask(nslab * tq, tq, tk)

    def score(j):
        return _nt(qs_ref[...], k_ref[pl.ds(pl.multiple_of(j * tk, tk), tk), :])

    def pv(n):
        acc_ref[...] = sum(_nn(p_ref[j], v_ref[j * tk:(j + 1) * tk, :]) for j in range(n))

    l = _two_pass_softmax(i, nq, tk, score, pv, mask, s_ref, p_ref, mx_ref, l_ref)

    lv = lam_ref[...]
    lam = (jnp.exp(jnp.sum(lv[0:1] * lv[1:2], axis=1, keepdims=True))
           - jnp.exp(jnp.sum(lv[2:3] * lv[3:4], axis=1, keepdims=True)) + lam_init)
    o = acc_ref[...] / l
    out = jnp.zeros((tq, DIFF_W), F32)
    for hd in range(A_HEADS):
        sel = (lane >= A_V_DIM * hd) & (lane < A_V_DIM * (hd + 1))
        oh = o[hd * tq:(hd + 1) * tq] - lam * o[(A_HEADS + hd) * tq:(A_HEADS + hd + 1) * tq]
        ms = jnp.sum(jnp.where(sel, oh * oh, 0.0), axis=1, keepdims=True) * (1.0 / A_V_DIM)
        out = out + jnp.where(sel, oh * lax.rsqrt(ms + EPS), 0.0)
    o_ref[...] = (out * g_ref[...] * (1.0 - lam_init)).astype(BF16)


def _diff_attn(qd, kd, vd, lamv, g4, lam_init, bsz, seq):
    tq = min(256, seq)
    nq = seq // tq
    nslab = 2 * A_HEADS
    rows = nslab * tq
    return pl.pallas_call(
        functools.partial(_diff_attn_body, tq, nq, lam_init),
        grid=(bsz, nq),
        in_specs=[pl.BlockSpec((tq, DIFF_W), lambda b, i: (b * nq + i, 0)),
                  pl.BlockSpec((seq, DIFF_W), lambda b, i: (b, 0)),
                  pl.BlockSpec((seq, DIFF_W), lambda b, i: (b, 0)),
                  pl.BlockSpec(lamv.shape, lambda b, i: (0, 0)),
                  pl.BlockSpec(g4.shape, lambda b, i: (0, 0))],
        out_specs=pl.BlockSpec((tq, DIFF_W), lambda b, i: (b * nq + i, 0)),
        out_shape=jax.ShapeDtypeStruct((bsz * seq, DIFF_W), BF16),
        scratch_shapes=[pltpu.VMEM((rows, DIFF_W), BF16), pltpu.VMEM((nq, rows, tq), F32),
                        pltpu.VMEM((nq, rows, tq), BF16), pltpu.VMEM((rows, LANES), F32),
                        pltpu.VMEM((rows, LANES), F32), pltpu.VMEM((rows, DIFF_W), F32)],
        compiler_params=_cparams(("arbitrary", "arbitrary")),
        name="diff_attn",
    )(qd, kd, vd, lamv, g4)


def _mla_attn_body(tq, nq, q_ref, k_ref, v_ref, o_ref, s_ref, p_ref, mx_ref, l_ref, acc1_ref, acc2_ref):
    i = pl.program_id(1)
    tk = tq
    mask = _chunk_mask(C_HEADS * tq, tq, tk)
    g1 = 4

    def score(j):
        kt = k_ref[pl.ds(pl.multiple_of(j * tk, tk), tk), :]
        return jnp.concatenate([_nt(q_ref[:, MLA_SLAB * hd:MLA_SLAB * (hd + 1)], kt[:, MLA_SLAB * hd:MLA_SLAB * (hd + 1)])
                                for hd in range(C_HEADS)], axis=0)

    def pv(n):
        acc1_ref[...] = sum(_nn(p_ref[j, 0:g1 * tq, :], v_ref[j * tk:(j + 1) * tk, 0:256]) for j in range(n))
        acc2_ref[...] = sum(_nn(p_ref[j, g1 * tq:, :], v_ref[j * tk:(j + 1) * tk, 256:MLA_V_W]) for j in range(n))

    l = _two_pass_softmax(i, nq, tk, score, pv, mask, s_ref, p_ref, mx_ref, l_ref)

    linv = 1.0 / l
    o1 = acc1_ref[...] * linv[0:g1 * tq]
    o2 = acc2_ref[...] * linv[g1 * tq:]
    lane1 = lax.broadcasted_iota(jnp.int32, (1, 256), 1)
    lane2 = lax.broadcasted_iota(jnp.int32, (1, 128), 1)
    out1 = jnp.zeros((tq, 256), F32)
    for hd in range(g1):
        out1 = out1 + jnp.where((lane1 >= C_V_DIM * hd) & (lane1 < C_V_DIM * (hd + 1)), o1[hd * tq:(hd + 1) * tq], 0.0)
    out2 = jnp.zeros((tq, 128), F32)
    for hd in range(C_HEADS - g1):
        out2 = out2 + jnp.where((lane2 >= C_V_DIM * hd) & (lane2 < C_V_DIM * (hd + 1)), o2[hd * tq:(hd + 1) * tq], 0.0)
    o_ref[...] = jnp.concatenate([out1, out2], axis=1).astype(BF16)


def _mla_attn(qc, kc, vc, bsz, seq):
    tq = min(256, seq)
    nq = seq // tq
    return pl.pallas_call(
        functools.partial(_mla_attn_body, tq, nq),
        grid=(bsz, nq),
        in_specs=[pl.BlockSpec((tq, MLA_QK_W), lambda b, i: (b * nq + i, 0)),
                  pl.BlockSpec((seq, MLA_QK_W), lambda b, i: (b, 0)),
                  pl.BlockSpec((seq, MLA_V_W), lambda b, i: (b, 0))],
        out_specs=pl.BlockSpec((tq, MLA_V_W), lambda b, i: (b * nq + i, 0)),
        out_shape=jax.ShapeDtypeStruct((bsz * seq, MLA_V_W), BF16),
        scratch_shapes=[pltpu.VMEM((nq, C_HEADS * tq, tq), F32), pltpu.VMEM((nq, C_HEADS * tq, tq), BF16),
                        pltpu.VMEM((C_HEADS * tq, LANES), F32), pltpu.VMEM((C_HEADS * tq, LANES), F32),
                        pltpu.VMEM((4 * tq, 256), F32), pltpu.VMEM((2 * tq, 128), F32)],
        compiler_params=_cparams(("arbitrary", "arbitrary")),
        name="mla_attn",
    )(qc, kc, vc)


def _mlstm_body(nc, q_ref, k_ref, v_ref, og_ref, gates_ref, gbias_ref, gout_ref, o_ref, ct_ref, m_ref):
    ct_ref[...] = jnp.zeros(ct_ref.shape, F32)
    m_ref[...] = jnp.zeros(m_ref.shape, F32)
    lane = lax.broadcasted_iota(jnp.int32, (1, LANES), 1)
    ti = lax.broadcasted_iota(jnp.int32, (CHUNK, CHUNK), 0)
    si = lax.broadcasted_iota(jnp.int32, (CHUNK, CHUNK), 1)
    tril = si <= ti
    eye = si == ti
    eye128 = (lax.broadcasted_iota(jnp.int32, (LANES, LANES), 0)
              == lax.broadcasted_iota(jnp.int32, (LANES, LANES), 1)).astype(BF16)
    gbias = gbias_ref[...]
    gout = gout_ref[...]

    def chunk(c, carry):
        rows = pl.ds(pl.multiple_of(c * CHUNK, CHUNK), CHUNK)
        gch = gates_ref[0, c] + gbias
        ig = gch[0:8]
        fg = gch[8:16]
        lf = jnp.minimum(fg, 0.0) - jnp.log(1.0 + jnp.exp(-jnp.abs(fg)))
        b = jnp.where(lane < CHUNK, lf, 0.0)
        for sh in (1, 2, 4, 8, 16, 32):
            b = b + jnp.where(lane >= sh, pltpu.roll(b, sh, 1), 0.0)
        a = ig - b
        a_last = jnp.max(jnp.where(lane < CHUNK, a, NEG_BIG), axis=1, keepdims=True)
        b_last = jnp.sum(jnp.where(lane == CHUNK - 1, b, 0.0), axis=1, keepdims=True)

        for pr in range(B_HEADS // 2):
            cols = slice(LANES * pr, LANES * (pr + 1))
            qw = q_ref[rows, cols]
            kw = k_ref[rows, cols]
            vw = v_ref[rows, cols]
            kt = _nt(eye128, kw)
            ctp = ct_ref[pr]
            ctb = ctp.astype(BF16)
            outs = []
            for hh in range(2):
                hd = 2 * pr + hh
                half = (lane < CHUNK) if hh == 0 else (lane >= CHUNK)
                onecol = (lane == CHUNK) if hh == 0 else (lane == 0)
                qmk = jnp.where(half, qw, jnp.zeros_like(qw))
                vaug = jnp.where(half, vw, jnp.where(onecol, 1.0, 0.0).astype(BF16))
                a_h = a[hd:hd + 1, 0:CHUNK]
                b_h = b[hd:hd + 1, 0:CHUNK]
                a_col = jnp.max(jnp.where(tril, a_h, NEG_BIG), axis=1, keepdims=True)
                b_col = jnp.sum(jnp.where(eye, b_h, 0.0), axis=1, keepdims=True)
                w0 = jnp.where(tril, jnp.exp(a_h - a_col), 0.0)
                sc0 = (_nt(qmk, kw) * w0).astype(BF16)
                x0 = _nn(sc0, vaug)
                xi = _nn(qmk, ctb)
                m = m_ref[hd:hd + 1, 0:1]
                m_col = jnp.maximum(a_col, m)
                xs = jnp.exp(a_col - m_col) * x0 + jnp.exp(m - m_col) * xi
                den = jnp.sum(jnp.where(onecol, xs, 0.0), axis=1, keepdims=True)
                hv = xs / jnp.maximum(jnp.abs(den), jnp.exp(-b_col - m_col))
                ms = jnp.sum(jnp.where(half, hv * hv, 0.0), axis=1, keepdims=True) * (1.0 / B_DIM)
                outs.append(hv * lax.rsqrt(ms + EPS))
                al = a_last[hd:hd + 1]
                m_l = jnp.maximum(al, m)
                kwt = (kt[CHUNK * hh:CHUNK * (hh + 1), :] * jnp.exp(a_h - al)).astype(BF16)
                dct = _nn(kwt, vaug)
                ct_ref[pr, CHUNK * hh:CHUNK * (hh + 1), :] = (jnp.exp(m - m_l) * ctp[CHUNK * hh:CHUNK * (hh + 1)]
                                                              + jnp.exp(al - m_l) * dct)
                m_ref[hd:hd + 1, :] = jnp.broadcast_to(b_last[hd:hd + 1] + m_l, (1, LANES))
            hn = jnp.where(lane < CHUNK, outs[0], outs[1])
            og = og_ref[rows, cols].astype(F32)
            o_ref[rows, cols] = (hn * gout[:, cols] * _sigmoid(og)).astype(BF16)
        return carry

    lax.fori_loop(0, nc, chunk, 0)


def _mlstm(qm, km, vm, om, gates, gbias, gout, bsz, seq):
    nc = seq // CHUNK
    blk = pl.BlockSpec((seq, MLSTM_W), lambda b: (b, 0))
    return pl.pallas_call(
        functools.partial(_mlstm_body, nc),
        grid=(bsz,),
        in_specs=[blk, blk, blk, blk,
                  pl.BlockSpec((1, nc, 16, LANES), lambda b: (b, 0, 0, 0)),
                  pl.BlockSpec(gbias.shape, lambda b: (0, 0)),
                  pl.BlockSpec(gout.shape, lambda b: (0, 0))],
        out_specs=blk,
        out_shape=jax.ShapeDtypeStruct((bsz * seq, MLSTM_W), BF16),
        scratch_shapes=[pltpu.VMEM((B_HEADS // 2, LANES, LANES), F32), pltpu.VMEM((8, LANES), F32)],
        compiler_params=_cparams(("arbitrary",)),
        name="mlstm",
    )(qm, km, vm, om, gates, gbias, gout)


def _mix_out_body(with_router, tm,
                  oa_ref, ob_ref, oc_ref, wout_ref, x_ref, gate_ref, g1_ref, shift_ref, scale_ref, g2_ref, *rest):
    if with_router:
        wr_ref, br_ref, xo_ref, h_ref, ri_ref, rg_ref, cnt_ref, run_ref = rest
    else:
        xo_ref, h_ref = rest
    na = A_HEADS * A_V_DIM
    y = (_nn(oa_ref[...], wout_ref[0:na, :]) + _nn(ob_ref[...], wout_ref[na:na + MLSTM_W, :])
         + _nn(oc_ref[...], wout_ref[na + MLSTM_W:, :]))
    xn = x_ref[...] + gate_ref[0] * (_rms(y, D_MODEL) * g1_ref[...])
    xo_ref[...] = xn
    h = _rms(xn, D_MODEL) * g2_ref[...]
    h = h * (1.0 + scale_ref[0]) + shift_ref[0]
    h_ref[...] = h.astype(BF16)
    if with_router:
        @pl.when(pl.program_id(0) == 0)
        def _():
            run_ref[...] = jnp.zeros(run_ref.shape, F32)

        lane = lax.broadcasted_iota(jnp.int32, (1, LANES), 1)
        logits = jnp.dot(h, wr_ref[...], precision=HIGHEST, preferred_element_type=F32) + br_ref[...]
        lg = jnp.where(lane < N_EXPERTS, logits, NEG_BIG)
        v1 = jnp.max(lg, axis=1, keepdims=True)
        i1 = jnp.min(jnp.where(lg == v1, lane, LANES), axis=1, keepdims=True)
        lg2 = jnp.where(lane == i1, NEG_BIG, lg)
        v2 = jnp.max(lg2, axis=1, keepdims=True)
        i2 = jnp.min(jnp.where(lg2 == v2, lane, LANES), axis=1, keepdims=True)
        gt1 = 1.0 / (1.0 + jnp.exp(v2 - v1))
        oh1 = lane == i1
        oh2 = lane == i2
        cnt = jnp.where(oh1 | oh2, 1.0, 0.0)
        ri_ = lax.broadcasted_iota(jnp.int32, (tm, tm), 0)
        ci_ = lax.broadcasted_iota(jnp.int32, (tm, tm), 1)
        lower = jnp.where(ci_ < ri_, 1.0, 0.0).astype(BF16)
        before = _nn(lower, cnt.astype(BF16)) + run_ref[0:1, :]
        r1 = jnp.sum(jnp.where(oh1, before, 0.0), axis=1, keepdims=True)
        r2 = jnp.sum(jnp.where(oh2, before, 0.0), axis=1, keepdims=True)
        run_ref[...] = run_ref[...] + jnp.sum(cnt, axis=0, keepdims=True)
        ri_ref[...] = jnp.where(lane == 0, i1, jnp.where(lane == 1, i2, jnp.where(
            lane == 2, r1.astype(jnp.int32), jnp.where(lane == 3, r2.astype(jnp.int32), 0))))
        rg_ref[...] = jnp.where(lane == 0, gt1, jnp.where(lane == 1, 1.0 - gt1, 0.0))
        cnt_ref[...] = run_ref[...]


def _mix_out(oa, ob, oc, wout, x2, gate, g1, shift, scale, g2, router, bsz, seq):
    n = bsz * seq
    d = D_MODEL
    tm = min(512, seq)
    tps = seq // tm
    row = lambda w: pl.BlockSpec((tm, w), lambda r: (r, 0))
    full = lambda a: pl.BlockSpec(a.shape, lambda r: (0,) * a.ndim)
    mod = pl.BlockSpec((1, 1, d), lambda r: (r // tps, 0, 0))
    in_specs = [row(oa.shape[1]), row(ob.shape[1]), row(oc.shape[1]), full(wout), row(d), mod, full(g1), mod, mod, full(g2)]
    args = [oa, ob, oc, wout, x2, gate, g1, shift, scale, g2]
    out_shape = [jax.ShapeDtypeStruct((n, d), F32), jax.ShapeDtypeStruct((n, d), BF16)]
    out_specs = [row(d), row(d)]
    scratch = []
    if router is not None:
        wr, br = router
        in_specs += [full(wr), full(br)]
        args += [wr, br]
        out_shape += [jax.ShapeDtypeStruct((n, LANES), jnp.int32), jax.ShapeDtypeStruct((n, LANES), F32),
                      jax.ShapeDtypeStruct((8, LANES), F32)]
        out_specs += [row(LANES), row(LANES), pl.BlockSpec((8, LANES), lambda r: (0, 0))]
        scratch = [pltpu.VMEM((8, LANES), F32)]
    return pl.pallas_call(
        functools.partial(_mix_out_body, router is not None, tm),
        grid=(n // tm,),
        in_specs=in_specs,
        out_specs=out_specs,
        out_shape=out_shape,
        scratch_shapes=scratch,
        compiler_params=_cparams(("arbitrary",)),
        name="mix_out_router" if router is not None else "mix_out",
    )(*args)


def _ffn_dense_body(h_ref, wg_ref, wu_ref, wd_ref, x_ref, gate_ref, g_ref, o_ref, acc_ref):
    j = pl.program_id(1)

    @pl.when(j == 0)
    def _():
        acc_ref[...] = jnp.zeros(acc_ref.shape, F32)

    h = h_ref[...]
    a = _nn(h, wg_ref[...])
    u = _nn(h, wu_ref[...])
    acc_ref[...] += _nn((a * _sigmoid(a) * u).astype(BF16), wd_ref[...])

    @pl.when(j == pl.num_programs(1) - 1)
    def _():
        o_ref[...] = x_ref[...] + gate_ref[0] * (_rms(acc_ref[...], D_MODEL) * g_ref[...])


def _ffn_dense(h, wg, wu, wd, x2, gate, g, bsz, seq):
    n = bsz * seq
    d = D_MODEL
    tm = min(1024, seq)
    tps = seq // tm
    tf = 512
    return pl.pallas_call(
        _ffn_dense_body,
        grid=(n // tm, D_FF // tf),
        in_specs=[pl.BlockSpec((tm, d), lambda i, j: (i, 0)),
                  pl.BlockSpec((d, tf), lambda i, j: (0, j)),
                  pl.BlockSpec((d, tf), lambda i, j: (0, j)),
                  pl.BlockSpec((tf, d), lambda i, j: (j, 0)),
                  pl.BlockSpec((tm, d), lambda i, j: (i, 0)),
                  pl.BlockSpec((1, 1, d), lambda i, j: (i // tps, 0, 0)),
                  pl.BlockSpec(g.shape, lambda i, j: (0, 0))],
        out_specs=pl.BlockSpec((tm, d), lambda i, j: (i, 0)),
        out_shape=jax.ShapeDtypeStruct((n, d), F32),
        scratch_shapes=[pltpu.VMEM((tm, d), F32)],
        compiler_params=_cparams(("arbitrary", "arbitrary")),
        name="ffn_dense",
    )(h, wg, wu, wd, x2, gate, g)


def _ffn_expert_body(blk_e_ref, nvalid_ref, x_ref, wg_ref, wu_ref, wd_ref, o_ref, acc_ref):
    i = pl.program_id(0)
    j = pl.program_id(1)
    valid = i < nvalid_ref[0]

    @pl.when(j == 0)
    def _():
        acc_ref[...] = jnp.zeros(acc_ref.shape, F32)

    @pl.when(valid)
    def _():
        xb = x_ref[...]
        a = _nn(xb, wg_ref[0])
        u = _nn(xb, wu_ref[0])
        acc_ref[...] += _nn((a * _sigmoid(a) * u).astype(BF16), wd_ref[0])

    @pl.when(j == pl.num_programs(1) - 1)
    def _():
        o_ref[...] = acc_ref[...]


def _ffn_expert(xb, wg, wu, wd, blk_e, nvalid, tmb):
    p, d = xb.shape
    nb = p // tmb
    tf = 512
    nf = D_FF // tf

    def jeff(i, j, nv):
        return jnp.where(i < nv[0], j, nf - 1)

    grid_spec = pltpu.PrefetchScalarGridSpec(
        num_scalar_prefetch=2,
        grid=(nb, nf),
        in_specs=[pl.BlockSpec((tmb, d), lambda i, j, be, nv: (i, 0)),
                  pl.BlockSpec((1, d, tf), lambda i, j, be, nv: (be[i], 0, jeff(i, j, nv))),
                  pl.BlockSpec((1, d, tf), lambda i, j, be, nv: (be[i], 0, jeff(i, j, nv))),
                  pl.BlockSpec((1, tf, d), lambda i, j, be, nv: (be[i], jeff(i, j, nv), 0))],
        out_specs=pl.BlockSpec((tmb, d), lambda i, j, be, nv: (i, 0)),
        scratch_shapes=[pltpu.VMEM((tmb, d), F32)],
    )
    return pl.pallas_call(
        _ffn_expert_body,
        grid_spec=grid_spec,
        out_shape=jax.ShapeDtypeStruct((p, d), F32),
        compiler_params=_cparams(("arbitrary", "arbitrary")),
        name="ffn_expert",
    )(blk_e, nvalid, xb, wg, wu, wd)


def _moe_combine_body(y1_ref, y2_ref, rg_ref, x_ref, gate_ref, g_ref, o_ref):
    rg = rg_ref[...]
    y = rg[:, 0:1] * y1_ref[...] + rg[:, 1:2] * y2_ref[...]
    o_ref[...] = x_ref[...] + gate_ref[0] * (_rms(y, D_MODEL) * g_ref[...])


def _moe_combine(y1, y2, rg, x2, gate, g, bsz, seq):
    n = bsz * seq
    d = D_MODEL
    tm = min(512, seq)
    tps = seq // tm
    row = lambda w: pl.BlockSpec((tm, w), lambda r: (r, 0))
    return pl.pallas_call(
        _moe_combine_body,
        grid=(n // tm,),
        in_specs=[row(d), row(d), row(LANES), row(d), pl.BlockSpec((1, 1, d), lambda r: (r // tps, 0, 0)),
                  pl.BlockSpec(g.shape, lambda r: (0, 0))],
        out_specs=row(d),
        out_shape=jax.ShapeDtypeStruct((n, d), F32),
        compiler_params=_cparams(("arbitrary",)),
        name="moe_combine",
    )(y1, y2, rg, x2, gate, g)


def _moe_ffn(h, ri, rg, counts, wg, wu, wd, x2, gate, g, bsz, seq):
    n = bsz * seq
    tmb = min(1024, n // 8)
    cnt = counts[0, :N_EXPERTS].astype(jnp.int32)
    padded = ((cnt + tmb - 1) // tmb) * tmb
    pend = jnp.cumsum(padded)
    pstart = pend - padded
    dest1 = pstart[ri[:, 0]] + ri[:, 2]
    dest2 = pstart[ri[:, 1]] + ri[:, 3]
    nb = (2 * n) // tmb + N_EXPERTS
    p = nb * tmb
    tok = jnp.arange(n, dtype=jnp.int32)
    buf_tok = jnp.zeros((p,), jnp.int32).at[dest1].set(tok).at[dest2].set(tok)
    nvalid = (pend[-1] // tmb).astype(jnp.int32).reshape(1)
    blk = jnp.minimum(jnp.arange(nb, dtype=jnp.int32), nvalid[0] - 1) * tmb
    blk_e = jnp.minimum(jnp.sum(blk[:, None] >= pend[None, :], axis=1), N_EXPERTS - 1).astype(jnp.int32)
    xb = jnp.take(h, buf_tok, axis=0)
    yb = _ffn_expert(xb, wg, wu, wd, blk_e, nvalid, tmb)
    y1 = jnp.take(yb, dest1, axis=0)
    y2 = jnp.take(yb, dest2, axis=0)
    return _moe_combine(y1, y2, rg, x2, gate, g, bsz, seq)


def _rot_cols(w):
    half = C_ROPE // 2
    return jnp.concatenate([-w[:, half:], w[:, :half]], axis=1)


def _prep_w_in(w_in):
    d = w_in.shape[0]
    z = lambda k: jnp.zeros((d, k), w_in.dtype)
    gates = w_in[:, 2304:2316]
    cq = w_in[:, 2316:2508]
    ckv = w_in[:, 2508:2636]
    kr = w_in[:, 2636:2668]
    cols = [w_in[:, :2304], cq, z(64), ckv,
            gates, z(C_NOPE - 12), kr, z(LANES - C_NOPE - C_ROPE),
            z(C_NOPE), _rot_cols(kr), z(LANES - C_NOPE - C_ROPE)]
    out = jnp.concatenate(cols, axis=1).astype(BF16)
    assert out.shape[1] == IN_COLS
    return out


def _prep_w_uq(w_uq):
    w = w_uq.reshape(C_Q_LORA, C_HEADS, C_NOPE + C_ROPE)
    nope, rp = w[..., :C_NOPE], w[..., C_NOPE:]
    pad = jnp.zeros((C_Q_LORA, C_HEADS, MLA_SLAB - C_NOPE - C_ROPE), w.dtype)
    rot = jnp.concatenate([-rp[..., C_ROPE // 2:], rp[..., :C_ROPE // 2]], axis=-1)
    wq = jnp.concatenate([nope, rp, pad], axis=-1).reshape(C_Q_LORA, MLA_QK_W)
    wqr = jnp.concatenate([jnp.zeros_like(nope), rot, pad], axis=-1).reshape(C_Q_LORA, MLA_QK_W)
    rowpad = jnp.zeros((256 - C_Q_LORA, MLA_QK_W), w.dtype)
    return (jnp.concatenate([wq, rowpad], axis=0).astype(BF16), jnp.concatenate([wqr, rowpad], axis=0).astype(BF16))


def _prep_w_ukv(w_ukv):
    w = w_ukv.reshape(C_KV_LORA, C_HEADS, C_NOPE + C_V_DIM)
    kn = jnp.concatenate([w[..., :C_NOPE], jnp.zeros((C_KV_LORA, C_HEADS, MLA_SLAB - C_NOPE), w.dtype)], axis=-1)
    v = w[..., C_NOPE:]
    return jnp.concatenate([kn.reshape(C_KV_LORA, MLA_QK_W), v.reshape(C_KV_LORA, MLA_V_W)], axis=1).astype(BF16)


def _rope_tables(positions):
    freqs = ROPE_THETA ** (-jnp.arange(0, C_ROPE, 2, dtype=F32) / C_ROPE)
    ang = positions.astype(F32).reshape(-1, 1) * freqs
    n = ang.shape[0]
    cos, sin = jnp.cos(ang), jnp.sin(ang)
    ones = lambda k: jnp.ones((n, k), F32)
    zeros = lambda k: jnp.zeros((n, k), F32)
    tail = MLA_SLAB - C_NOPE - C_ROPE
    return (jnp.concatenate([ones(C_NOPE), cos, cos, ones(tail)], axis=1),
            jnp.concatenate([zeros(C_NOPE), sin, sin, zeros(tail)], axis=1))


def kernel(x, c, positions, w_ada, b_ada, g_norm, w_in, w_out, lam_q1, lam_k1, lam_q2, lam_k2, g_diff, w_conv, b_conv, b_igate, b_fgate, g_mlstm, g_q_lat, g_kv_lat, w_uq, w_ukv, w_gate_d, w_up_d, w_down_d, w_router, b_router, w_gate_e, w_up_e, w_down_e):
    bsz, seq, d = x.shape
    n = bsz * seq
    nc = seq // CHUNK
    depth = w_in.shape[0]
    mods = _ada_all(c, w_ada, b_ada)
    cos_t, sin_t = _rope_tables(positions)
    x2 = x.reshape(n, d)
    row1 = lambda v: v.reshape(1, -1).astype(F32)

    for l in range(depth):
        mod = lambda k, j: mods[2 * l + k, j].reshape(bsz, 1, d)
        wq, wqr = _prep_w_uq(w_uq[l])
        gq = jnp.concatenate([g_q_lat[l], jnp.zeros((256 - C_Q_LORA,), F32)]).reshape(1, 256)
        (qd, kd, vd, qm, km, vm, om, gt, qc, kc, vc) = _mix_in(
            x2, mod(0, 0), mod(0, 1), row1(g_norm[l, 0]), _prep_w_in(w_in[l]), w_conv[l], row1(b_conv[l]),
            gq, row1(g_kv_lat[l]), wq, wqr, _prep_w_ukv(w_ukv[l]), cos_t, sin_t, bsz, seq)

        lam_init = 0.8 - 0.6 * math.exp(-0.3 * l)
        lamv = jnp.zeros((8, LANES), F32).at[0:4, 0:A_QK_DIM].set(jnp.stack([lam_q1[l], lam_k1[l], lam_q2[l], lam_k2[l]]))
        out_a = _diff_attn(qd, kd, vd, lamv, row1(jnp.tile(g_diff[l], A_HEADS)), lam_init, bsz, seq)

        z2 = jnp.zeros((bsz, 2, seq), F32)
        g16 = jnp.concatenate([gt[:, 0:6], z2, gt[:, 6:12], z2], axis=1)
        gates = jnp.pad(g16.reshape(bsz, 16, nc, CHUNK).transpose(0, 2, 1, 3), ((0, 0), (0, 0), (0, 0), (0, LANES - CHUNK)))
        zb = jnp.zeros((2,), F32)
        gbias = jnp.broadcast_to(jnp.concatenate([b_igate[l], zb, b_fgate[l], zb])[:, None], (16, LANES)).astype(F32)
        out_b = _mlstm(qm, km, vm, om, gates, gbias, row1(g_mlstm[l]), bsz, seq)

        out_c = _mla_attn(qc, kc, vc, bsz, seq)

        moe = (l % 2 == 1)
        router = None
        if moe:
            wr = jnp.pad(w_router[l // 2], ((0, 0), (0, LANES - N_EXPERTS)))
            br = jnp.pad(b_router[l // 2], (0, LANES - N_EXPERTS)).reshape(1, LANES)
            router = (wr, br)
        res = _mix_out(out_a, out_b, out_c, w_out[l].astype(BF16), x2, mod(0, 2), row1(g_norm[l, 1]),
                       mod(1, 0), mod(1, 1), row1(g_norm[l, 2]), router, bsz, seq)
        if moe:
            x2, h2, ri, rg, counts = res
            x2 = _moe_ffn(h2, ri, rg, counts, w_gate_e[l // 2].astype(BF16), w_up_e[l // 2].astype(BF16),
                          w_down_e[l // 2].astype(BF16), x2, mod(1, 2), row1(g_norm[l, 3]), bsz, seq)
        else:
            x2, h2 = res
            x2 = _ffn_dense(h2, w_gate_d[l // 2].astype(BF16), w_up_d[l // 2].astype(BF16),
                            w_down_d[l // 2].astype(BF16), x2, mod(1, 2), row1(g_norm[l, 3]), bsz, seq)
    return x2.reshape(bsz, seq, d)
```

```python
import functools
import math

import jax
import jax.numpy as jnp
import numpy as np
from jax import lax
from jax.experimental import pallas as pl
from jax.experimental.pallas import tpu as pltpu

F32 = jnp.float32
BF16 = jnp.bfloat16
HIGHEST = lax.Precision.HIGHEST

D_MODEL = 1024
DEPTH = 2
CHUNK = 64
EPS = 1e-6
A_HEADS, A_QK_DIM, A_V_DIM = 4, 32, 64
B_HEADS, B_DIM, CONV_WIDTH = 6, 64, 4
C_HEADS, C_Q_LORA, C_KV_LORA, C_NOPE, C_ROPE, C_V_DIM = 6, 192, 128, 64, 32, 64
ROPE_THETA = 10000.0
D_FF = 3584
N_EXPERTS = 8
LOG2E = 1.4426950408889634
NEG_BIG = -1e30

LANES = 128
DIFF_W = 2 * A_HEADS * A_QK_DIM
MLSTM_W = B_HEADS * B_DIM
MLA_SLAB = 128
MLA_QK_W = C_HEADS * MLA_SLAB
MLA_V_W = C_HEADS * C_V_DIM

COL_DIFF = 0
COL_MLSTM = 768
COL_CQ = 2304
COL_CKV = 2560
COL_X = 2688
COL_Y = 2816
IN_COLS = 2944

VMEM_LIMIT = 56 * 1024 * 1024


def _cparams(sem):
    return pltpu.CompilerParams(dimension_semantics=sem, vmem_limit_bytes=VMEM_LIMIT)


def _nt(a, b):
    return lax.dot_general(a, b, (((1,), (1,)), ((), ())), preferred_element_type=F32)


def _nn(a, b):
    return jnp.dot(a, b, preferred_element_type=F32)


def _sigmoid(v):
    return 1.0 / (1.0 + jnp.exp(-v))


def _rms(v, n):
    return v * lax.rsqrt(jnp.sum(v * v, axis=-1, keepdims=True) * (1.0 / n) + EPS)


def _ada_body(c_ref, w_ref, b_ref, o_ref):
    c = c_ref[...]
    sc = c * _sigmoid(c)
    o_ref[0, 0] = jnp.dot(sc, w_ref[0], precision=HIGHEST, preferred_element_type=F32) + b_ref[0]


def _ada_all(c, w_ada, b_ada):
    bsz, d = c.shape
    n = w_ada.shape[0] * w_ada.shape[1]
    w = w_ada.reshape(n, d, 3 * d)
    b = b_ada.reshape(n, 1, 3 * d)
    return pl.pallas_call(
        _ada_body,
        grid=(n, 3),
        in_specs=[pl.BlockSpec((bsz, d), lambda i, j: (0, 0)),
                  pl.BlockSpec((1, d, d), lambda i, j: (i, 0, j)),
                  pl.BlockSpec((1, 1, d), lambda i, j: (i, 0, j))],
        out_specs=pl.BlockSpec((1, 1, bsz, d), lambda i, j: (i, j, 0, 0)),
        out_shape=jax.ShapeDtypeStruct((n, 3, bsz, d), F32),
        compiler_params=_cparams(("arbitrary", "arbitrary")),
        name="ada_mod",
    )(c, w, b)


def _mix_in_body(tiles_per_seq, tm,
                 x_ref, shift_ref, scale_ref, g_ref, win_ref, wconv_ref, bconv_ref, gq_ref, gkv_ref,
                 wq_ref, wqr_ref, wkv_ref, cos_ref, sin_ref,
                 qd_ref, kd_ref, vd_ref, qm_ref, km_ref, vm_ref, om_ref, gt_ref, qc_ref, kc_ref, vc_ref,
                 carry_ref):
    r = pl.program_id(0)
    x = x_ref[...]
    h = _rms(x, D_MODEL) * g_ref[...]
    h = h * (1.0 + scale_ref[0]) + shift_ref[0]
    p = _nn(h.astype(BF16), win_ref[...])

    qd_ref[...] = (p[:, 0:256] * (A_QK_DIM ** -0.5 * LOG2E)).astype(BF16)
    kd_ref[...] = p[:, 256:512].astype(BF16)
    vd_ref[...] = p[:, 512:768].astype(BF16)

    @pl.when(r % tiles_per_seq == 0)
    def _():
        carry_ref[0:8, :] = jnp.zeros((8, 2 * MLSTM_W), F32)

    carry_ref[8:8 + tm, :] = p[:, COL_MLSTM:COL_MLSTM + 2 * MLSTM_W]
    w = wconv_ref[...]
    conv = bconv_ref[...] + carry_ref[8:8 + tm, :] * w[3:4, :]
    for j in range(CONV_WIDTH - 1):
        conv = conv + carry_ref[5 + j:5 + j + tm, :] * w[j:j + 1, :]
    carry_ref[0:8, :] = carry_ref[tm:tm + 8, :]
    qk = conv * _sigmoid(conv)
    qm_ref[...] = qk[:, 0:MLSTM_W].astype(BF16)
    km_ref[...] = (qk[:, MLSTM_W:2 * MLSTM_W] * (B_DIM ** -0.5)).astype(BF16)
    vm_ref[...] = p[:, COL_MLSTM + 2 * MLSTM_W:COL_MLSTM + 3 * MLSTM_W].astype(BF16)
    om_ref[...] = p[:, COL_MLSTM + 3 * MLSTM_W:COL_MLSTM + 4 * MLSTM_W].astype(BF16)

    xg = p[:, COL_X:COL_X + LANES]
    gt_ref[0] = xg.T[0:16, :]

    cq = p[:, COL_CQ:COL_CQ + 256]
    cqn = (cq * lax.rsqrt(jnp.sum(cq * cq, axis=-1, keepdims=True) * (1.0 / C_Q_LORA) + EPS) * gq_ref[...]).astype(BF16)
    ckv = p[:, COL_CKV:COL_CKV + C_KV_LORA]
    ckvn = (_rms(ckv, C_KV_LORA) * gkv_ref[...]).astype(BF16)
    cos_t = cos_ref[...]
    sin_t = sin_ref[...]
    cos6 = jnp.concatenate([cos_t] * C_HEADS, axis=1)
    sin6 = jnp.concatenate([sin_t] * C_HEADS, axis=1)
    qn = _nn(cqn, wq_ref[...])
    qr = _nn(cqn, wqr_ref[...])
    qc_ref[...] = ((qn * cos6 + qr * sin6) * ((C_NOPE + C_ROPE) ** -0.5 * LOG2E)).astype(BF16)
    kvp = _nn(ckvn, wkv_ref[...])
    lane = lax.broadcasted_iota(jnp.int32, (1, LANES), 1)
    rope_lanes = (lane >= C_NOPE) & (lane < C_NOPE + C_ROPE)
    yg = p[:, COL_Y:COL_Y + LANES]
    krope = jnp.where(rope_lanes, xg * cos_t + yg * sin_t, 0.0)
    kc_ref[...] = (kvp[:, 0:MLA_QK_W] + jnp.concatenate([krope] * C_HEADS, axis=1)).astype(BF16)
    vc_ref[...] = kvp[:, MLA_QK_W:MLA_QK_W + MLA_V_W].astype(BF16)


def _mix_in(x2, shift, scale, g, win, wconv, bconv, gq, gkv, wq, wqr, wkv, cos_t, sin_t, bsz, seq):
    n = bsz * seq
    tm = min(512, seq)
    tps = seq // tm
    d = D_MODEL
    row = lambda w: pl.BlockSpec((tm, w), lambda r: (r, 0))
    full = lambda a: pl.BlockSpec(a.shape, lambda r: (0,) * a.ndim)
    mod = pl.BlockSpec((1, 1, d), lambda r: (r // tps, 0, 0))
    out_widths = [DIFF_W, DIFF_W, DIFF_W, MLSTM_W, MLSTM_W, MLSTM_W, MLSTM_W]
    out_shape = [jax.ShapeDtypeStruct((n, w), BF16) for w in out_widths]
    out_specs = [row(w) for w in out_widths]
    out_shape.append(jax.ShapeDtypeStruct((bsz, 16, seq), F32))
    out_specs.append(pl.BlockSpec((1, 16, tm), lambda r: (r // tps, 0, r % tps)))
    for w in (MLA_QK_W, MLA_QK_W, MLA_V_W):
        out_shape.append(jax.ShapeDtypeStruct((n, w), BF16))
        out_specs.append(row(w))
    return pl.pallas_call(
        functools.partial(_mix_in_body, tps, tm),
        grid=(n // tm,),
        in_specs=[row(d), mod, mod, full(g), full(win), full(wconv), full(bconv), full(gq), full(gkv),
                  full(wq), full(wqr), full(wkv), row(LANES), row(LANES)],
        out_specs=out_specs,
        out_shape=out_shape,
        scratch_shapes=[pltpu.VMEM((tm + 8, 2 * MLSTM_W), F32)],
        compiler_params=_cparams(("arbitrary",)),
        name="mix_in",
    )(x2, shift, scale, g, win, wconv, bconv, gq, gkv, wq, wqr, wkv, cos_t, sin_t)


def _chunk_mask(tq, tk):
    ri = lax.broadcasted_iota(jnp.int32, (tq, tk), 0)
    ci = lax.broadcasted_iota(jnp.int32, (tq, tk), 1)
    return (ci >> 6) <= (ri >> 6)


def _two_pass_softmax(i, nq, tk, score_fn, pv_fn, mask, s_ref, p_ref, mx_ref, l_ref):
    rows = mx_ref.shape[0]
    tq = mask.shape[0]
    mx_ref[...] = jnp.full(mx_ref.shape, NEG_BIG, F32)

    def pass1(j, masked):
        s = score_fn(j)
        if masked:
            s = jnp.where(mask, s.reshape(rows // tq, tq, tk), NEG_BIG).reshape(rows, tk)
        s_ref[j] = s
        mx_ref[...] = jnp.maximum(mx_ref[...], jnp.maximum(s[:, 0:LANES], s[:, LANES:2 * LANES]))

    def body1(j, c):
        pass1(j, False)
        return c

    lax.fori_loop(0, i, body1, 0)
    pass1(i, True)
    mx_ref[...] = jnp.broadcast_to(jnp.max(mx_ref[...], axis=1, keepdims=True), mx_ref.shape)

    rb = 128

    def pass2_pv(n):
        for j in range(n):
            for r0 in range(0, rows, rb):
                s = s_ref[j, r0:r0 + rb, :]
                m = mx_ref[r0:r0 + rb, :]
                p0 = jnp.exp2(s[:, 0:LANES] - m)
                p1 = jnp.exp2(s[:, LANES:2 * LANES] - m)
                l_ref[r0:r0 + rb, :] = (p0 + p1) if j == 0 else (l_ref[r0:r0 + rb, :] + (p0 + p1))
                p_ref[j, r0:r0 + rb, :] = jnp.concatenate([p0, p1], axis=1).astype(BF16)
        pv_fn(n)

    for n in range(1, nq + 1):
        @pl.when(i == n - 1)
        def _():
            pass2_pv(n)
    return jnp.sum(l_ref[...], axis=1, keepdims=True)


def _diff_attn_body(tq, nq, lam_init, q_ref, k_ref, v_ref, lam_ref, g_ref, o_ref,
                    qs_ref, s_ref, p_ref, mx_ref, l_ref, acc_ref):
    i = pl.program_id(1)
    tk = tq
    nslab = 2 * A_HEADS
    q = q_ref[...]
    lane = lax.broadcasted_iota(jnp.int32, (1, DIFF_W), 1)
    zero = jnp.zeros_like(q)
    for s in range(nslab):
        qs_ref[s * tq:(s + 1) * tq, :] = jnp.where((lane >= A_QK_DIM * s) & (lane < A_QK_DIM * (s + 1)), q, zero)
    mask = _chunk_mask(tq, tk)

    def score(j):
        return _nt(qs_ref[...], k_ref[pl.ds(pl.multiple_of(j * tk, tk), tk), :])

    def pv(n):
        acc_ref[...] = sum(_nn(p_ref[j], v_ref[j * tk:(j + 1) * tk, :]) for j in range(n))

    l = _two_pass_softmax(i, nq, tk, score, pv, mask, s_ref, p_ref, mx_ref, l_ref)

    lv = lam_ref[...]
    lam = (jnp.exp(jnp.sum(lv[0:1] * lv[1:2], axis=1, keepdims=True))
           - jnp.exp(jnp.sum(lv[2:3] * lv[3:4], axis=1, keepdims=True)) + lam_init)
    o = acc_ref[...] / l
    out = jnp.zeros((tq, DIFF_W), F32)
    for hd in range(A_HEADS):
        sel = (lane >= A_V_DIM * hd) & (lane < A_V_DIM * (hd + 1))
        oh = o[hd * tq:(hd + 1) * tq] - lam * o[(A_HEADS + hd) * tq:(A_HEADS + hd + 1) * tq]
        ms = jnp.sum(jnp.where(sel, oh * oh, 0.0), axis=1, keepdims=True) * (1.0 / A_V_DIM)
        out = out + jnp.where(sel, oh * lax.rsqrt(ms + EPS), 0.0)
    o_ref[...] = (out * g_ref[...] * (1.0 - lam_init)).astype(BF16)


def _diff_attn(qd, kd, vd, lamv, g4, lam_init, bsz, seq):
    tq = min(256, seq)
    nq = seq // tq
    nslab = 2 * A_HEADS
    rows = nslab * tq
    return pl.pallas_call(
        functools.partial(_diff_attn_body, tq, nq, lam_init),
        grid=(bsz, nq),
        in_specs=[pl.BlockSpec((tq, DIFF_W), lambda b, i: (b * nq + i, 0)),
                  pl.BlockSpec((seq, DIFF_W), lambda b, i: (b, 0)),
                  pl.BlockSpec((seq, DIFF_W), lambda b, i: (b, 0)),
                  pl.BlockSpec(lamv.shape, lambda b, i: (0, 0)),
                  pl.BlockSpec(g4.shape, lambda b, i: (0, 0))],
        out_specs=pl.BlockSpec((tq, DIFF_W), lambda b, i: (b * nq + i, 0)),
        out_shape=jax.ShapeDtypeStruct((bsz * seq, DIFF_W), BF16),
        scratch_shapes=[pltpu.VMEM((rows, DIFF_W), BF16), pltpu.VMEM((nq, rows, tq), F32),
                        pltpu.VMEM((nq, rows, tq), BF16), pltpu.VMEM((rows, LANES), F32),
                        pltpu.VMEM((rows, LANES), F32), pltpu.VMEM((rows, DIFF_W), F32)],
        compiler_params=_cparams(("arbitrary", "arbitrary")),
        name="diff_attn",
    )(qd, kd, vd, lamv, g4)


def _mla_attn_body(tq, nq, q_ref, k_ref, v_ref, o_ref, s_ref, p_ref, mx_ref, l_ref, acc1_ref, acc2_ref):
    i = pl.program_id(1)
    tk = tq
    mask = _chunk_mask(tq, tk)
    g1 = 4

    def score(j):
        kt = k_ref[pl.ds(pl.multiple_of(j * tk, tk), tk), :]
        return jnp.concatenate([_nt(q_ref[:, MLA_SLAB * hd:MLA_SLAB * (hd + 1)], kt[:, MLA_SLAB * hd:MLA_SLAB * (hd + 1)])
                                for hd in range(C_HEADS)], axis=0)

    def pv(n):
        acc1_ref[...] = sum(_nn(p_ref[j, 0:g1 * tq, :], v_ref[j * tk:(j + 1) * tk, 0:256]) for j in range(n))
        acc2_ref[...] = sum(_nn(p_ref[j, g1 * tq:, :], v_ref[j * tk:(j + 1) * tk, 256:MLA_V_W]) for j in range(n))

    l = _two_pass_softmax(i, nq, tk, score, pv, mask, s_ref, p_ref, mx_ref, l_ref)

    linv = 1.0 / l
    o1 = acc1_ref[...] * linv[0:g1 * tq]
    o2 = acc2_ref[...] * linv[g1 * tq:]
    lane1 = lax.broadcasted_iota(jnp.int32, (1, 256), 1)
    lane2 = lax.broadcasted_iota(jnp.int32, (1, 128), 1)
    out1 = jnp.zeros((tq, 256), F32)
    for hd in range(g1):
        out1 = out1 + jnp.where((lane1 >= C_V_DIM * hd) & (lane1 < C_V_DIM * (hd + 1)), o1[hd * tq:(hd + 1) * tq], 0.0)
    out2 = jnp.zeros((tq, 128), F32)
    for hd in range(C_HEADS - g1):
        out2 = out2 + jnp.where((lane2 >= C_V_DIM * hd) & (lane2 < C_V_DIM * (hd + 1)), o2[hd * tq:(hd + 1) * tq], 0.0)
    o_ref[...] = jnp.concatenate([out1, out2], axis=1).astype(BF16)


def _mla_attn(qc, kc, vc, bsz, seq):
    tq = min(256, seq)
    nq = seq // tq
    return pl.pallas_call(
        functools.partial(_mla_attn_body, tq, nq),
        grid=(bsz, nq),
        in_specs=[pl.BlockSpec((tq, MLA_QK_W), lambda b, i: (b * nq + i, 0)),
                  pl.BlockSpec((seq, MLA_QK_W), lambda b, i: (b, 0)),
                  pl.BlockSpec((seq, MLA_V_W), lambda b, i: (b, 0))],
        out_specs=pl.BlockSpec((tq, MLA_V_W), lambda b, i: (b * nq + i, 0)),
        out_shape=jax.ShapeDtypeStruct((bsz * seq, MLA_V_W), BF16),
        scratch_shapes=[pltpu.VMEM((nq, C_HEADS * tq, tq), F32), pltpu.VMEM((nq, C_HEADS * tq, tq), BF16),
                        pltpu.VMEM((C_HEADS * tq, LANES), F32), pltpu.VMEM((C_HEADS * tq, LANES), F32),
                        pltpu.VMEM((4 * tq, 256), F32), pltpu.VMEM((2 * tq, 128), F32)],
        compiler_params=_cparams(("arbitrary", "arbitrary")),
        name="mla_attn",
    )(qc, kc, vc)


def _mlstm_body(nc, q_ref, k_ref, v_ref, og_ref, gates_ref, gbias_ref, gout_ref, o_ref,
                ct_ref, m_ref, a_ref, b_ref, al_ref, bl_ref):
    ct_ref[...] = jnp.zeros(ct_ref.shape, F32)
    m_ref[...] = jnp.zeros(m_ref.shape, F32)
    lane = lax.broadcasted_iota(jnp.int32, (1, LANES), 1)
    ti = lax.broadcasted_iota(jnp.int32, (CHUNK, CHUNK), 0)
    si = lax.broadcasted_iota(jnp.int32, (CHUNK, CHUNK), 1)
    tril = si <= ti
    eye = si == ti
    eye128 = (lax.broadcasted_iota(jnp.int32, (LANES, LANES), 0)
              == lax.broadcasted_iota(jnp.int32, (LANES, LANES), 1)).astype(BF16)
    gbias = gbias_ref[...]
    gout = gout_ref[...]

    g = gates_ref[0]
    ig = (g[:, 0:8, :] + gbias[0:8]).reshape(nc * 8, LANES)
    fg = (g[:, 8:16, :] + gbias[8:16]).reshape(nc * 8, LANES)
    lf = jnp.minimum(fg, 0.0) - jnp.log(1.0 + jnp.exp(-jnp.abs(fg)))
    b_all = jnp.where(lane < CHUNK, lf, 0.0)
    for sh in (1, 2, 4, 8, 16, 32):
        b_all = b_all + jnp.where(lane >= sh, pltpu.roll(b_all, sh, 1), 0.0)
    a_all = ig - b_all
    a_ref[...] = a_all.reshape(nc, 8, LANES)
    b_ref[...] = b_all.reshape(nc, 8, LANES)
    al_ref[...] = jnp.broadcast_to(jnp.max(jnp.where(lane < CHUNK, a_all, NEG_BIG), axis=1, keepdims=True),
                                   (nc * 8, LANES)).reshape(nc, 8, LANES)
    bl_ref[...] = jnp.broadcast_to(jnp.sum(jnp.where(lane == CHUNK - 1, b_all, 0.0), axis=1, keepdims=True),
                                   (nc * 8, LANES)).reshape(nc, 8, LANES)

    def chunk(c, carry):
        rows = pl.ds(pl.multiple_of(c * CHUNK, CHUNK), CHUNK)
        a = a_ref[c]
        b = b_ref[c]
        a_last = al_ref[c][:, 0:1]
        b_last = bl_ref[c][:, 0:1]

        for pr in range(B_HEADS // 2):
            cols = slice(LANES * pr, LANES * (pr + 1))
            qw = q_ref[rows, cols]
            kw = k_ref[rows, cols]
            vw = v_ref[rows, cols]
            kt = _nt(eye128, kw)
            ctp = ct_ref[pr]
            ctb = ctp.astype(BF16)
            outs = []
            for hh in range(2):
                hd = 2 * pr + hh
                half = (lane < CHUNK) if hh == 0 else (lane >= CHUNK)
                onecol = (lane == CHUNK) if hh == 0 else (lane == 0)
                qmk = jnp.where(half, qw, jnp.zeros_like(qw))
                vaug = jnp.where(half, vw, jnp.where(onecol, 1.0, 0.0).astype(BF16))
                a_h = a[hd:hd + 1, 0:CHUNK]
                b_h = b[hd:hd + 1, 0:CHUNK]
                a_col = jnp.max(jnp.where(tril, a_h, NEG_BIG), axis=1, keepdims=True)
                b_col = jnp.sum(jnp.where(eye, b_h, 0.0), axis=1, keepdims=True)
                w0 = jnp.where(tril, jnp.exp(a_h - a_col), 0.0)
                sc0 = (_nt(qmk, kw) * w0).astype(BF16)
                x0 = _nn(sc0, vaug)
                xi = _nn(qmk, ctb)
                m = m_ref[hd:hd + 1, 0:1]
                m_col = jnp.maximum(a_col, m)
                xs = jnp.exp(a_col - m_col) * x0 + jnp.exp(m - m_col) * xi
                den = jnp.sum(jnp.where(onecol, xs, 0.0), axis=1, keepdims=True)
                hv = xs / jnp.maximum(jnp.abs(den), jnp.exp(-b_col - m_col))
                ms = jnp.sum(jnp.where(half, hv * hv, 0.0), axis=1, keepdims=True) * (1.0 / B_DIM)
                outs.append(hv * lax.rsqrt(ms + EPS))
                al = a_last[hd:hd + 1]
                m_l = jnp.maximum(al, m)
                kwt = (kt[CHUNK * hh:CHUNK * (hh + 1), :] * jnp.exp(a_h - al)).astype(BF16)
                dct = _nn(kwt, vaug)
                ct_ref[pr, CHUNK * hh:CHUNK * (hh + 1), :] = (jnp.exp(m - m_l) * ctp[CHUNK * hh:CHUNK * (hh + 1)]
                                                              + jnp.exp(al - m_l) * dct)
                m_ref[hd:hd + 1, :] = jnp.broadcast_to(b_last[hd:hd + 1] + m_l, (1, LANES))
            hn = jnp.where(lane < CHUNK, outs[0], outs[1])
            og = og_ref[rows, cols].astype(F32)
            o_ref[rows, cols] = (hn * gout[:, cols] * _sigmoid(og)).astype(BF16)
        return carry

    lax.fori_loop(0, nc, chunk, 0, unroll=2)


def _mlstm(qm, km, vm, om, gates, gbias, gout, bsz, seq):
    nc = seq // CHUNK
    blk = pl.BlockSpec((seq, MLSTM_W), lambda b: (b, 0))
    return pl.pallas_call(
        functools.partial(_mlstm_body, nc),
        grid=(bsz,),
        in_specs=[blk, blk, blk, blk,
                  pl.BlockSpec((1, nc, 16, LANES), lambda b: (b, 0, 0, 0)),
                  pl.BlockSpec(gbias.shape, lambda b: (0, 0)),
                  pl.BlockSpec(gout.shape, lambda b: (0, 0))],
        out_specs=blk,
        out_shape=jax.ShapeDtypeStruct((bsz * seq, MLSTM_W), BF16),
        scratch_shapes=[pltpu.VMEM((B_HEADS // 2, LANES, LANES), F32), pltpu.VMEM((8, LANES), F32)]
        + [pltpu.VMEM((nc, 8, LANES), F32)] * 4,
        compiler_params=_cparams(("arbitrary",)),
        name="mlstm",
    )(qm, km, vm, om, gates, gbias, gout)


def _mix_out_body(with_router, tm,
                  oa_ref, ob_ref, oc_ref, wout_ref, x_ref, gate_ref, g1_ref, shift_ref, scale_ref, g2_ref, *rest):
    if with_router:
        wr_ref, br_ref, xo_ref, h_ref, ri_ref, rg_ref, cnt_ref, run_ref = rest
    else:
        xo_ref, h_ref = rest
    na = A_HEADS * A_V_DIM
    y = (_nn(oa_ref[...], wout_ref[0:na, :]) + _nn(ob_ref[...], wout_ref[na:na + MLSTM_W, :])
         + _nn(oc_ref[...], wout_ref[na + MLSTM_W:, :]))
    xn = x_ref[...] + gate_ref[0] * (_rms(y, D_MODEL) * g1_ref[...])
    xo_ref[...] = xn
    h = _rms(xn, D_MODEL) * g2_ref[...]
    h = h * (1.0 + scale_ref[0]) + shift_ref[0]
    h_ref[...] = h.astype(BF16)
    if with_router:
        @pl.when(pl.program_id(0) == 0)
        def _():
            run_ref[...] = jnp.zeros(run_ref.shape, F32)

        lane = lax.broadcasted_iota(jnp.int32, (1, LANES), 1)
        h_hi = h.astype(BF16)
        h_lo = (h - h_hi.astype(F32)).astype(BF16)
        part = _nn(h_hi, wr_ref[...]) + _nn(h_lo, wr_ref[...])
        logits = part + pltpu.roll(part, LANES - N_EXPERTS, 1) + br_ref[...]
        lg = jnp.where(lane < N_EXPERTS, logits, NEG_BIG)
        v1 = jnp.max(lg, axis=1, keepdims=True)
        i1 = jnp.min(jnp.where(lg == v1, lane, LANES), axis=1, keepdims=True)
        lg2 = jnp.where(lane == i1, NEG_BIG, lg)
        v2 = jnp.max(lg2, axis=1, keepdims=True)
        i2 = jnp.min(jnp.where(lg2 == v2, lane, LANES), axis=1, keepdims=True)
        gt1 = 1.0 / (1.0 + jnp.exp(v2 - v1))
        oh1 = lane == i1
        oh2 = lane == i2
        cnt = jnp.where(oh1 | oh2, 1.0, 0.0)
        ri_ = lax.broadcasted_iota(jnp.int32, (tm, tm), 0)
        ci_ = lax.broadcasted_iota(jnp.int32, (tm, tm), 1)
        lower = jnp.where(ci_ < ri_, 1.0, 0.0).astype(BF16)
        before = _nn(lower, cnt.astype(BF16)) + run_ref[0:1, :]
        r1 = jnp.sum(jnp.where(oh1, before, 0.0), axis=1, keepdims=True)
        r2 = jnp.sum(jnp.where(oh2, before, 0.0), axis=1, keepdims=True)
        run_ref[...] = run_ref[...] + jnp.sum(cnt, axis=0, keepdims=True)
        ri_ref[...] = jnp.where(lane == 0, i1, jnp.where(lane == 1, i2, jnp.where(
            lane == 2, r1.astype(jnp.int32), jnp.where(lane == 3, r2.astype(jnp.int32), 0))))
        rg_ref[...] = jnp.where(lane == 0, gt1, jnp.where(lane == 1, 1.0 - gt1, 0.0))
        cnt_ref[...] = run_ref[...]


def _mix_out(oa, ob, oc, wout, x2, gate, g1, shift, scale, g2, router, bsz, seq):
    n = bsz * seq
    d = D_MODEL
    tm = min(512, seq)
    tps = seq // tm
    row = lambda w: pl.BlockSpec((tm, w), lambda r: (r, 0))
    full = lambda a: pl.BlockSpec(a.shape, lambda r: (0,) * a.ndim)
    mod = pl.BlockSpec((1, 1, d), lambda r: (r // tps, 0, 0))
    in_specs = [row(oa.shape[1]), row(ob.shape[1]), row(oc.shape[1]), full(wout), row(d), mod, full(g1), mod, mod, full(g2)]
    args = [oa, ob, oc, wout, x2, gate, g1, shift, scale, g2]
    out_shape = [jax.ShapeDtypeStruct((n, d), F32), jax.ShapeDtypeStruct((n, d), BF16)]
    out_specs = [row(d), row(d)]
    scratch = []
    if router is not None:
        wr, br = router
        in_specs += [full(wr), full(br)]
        args += [wr, br]
        out_shape += [jax.ShapeDtypeStruct((n, LANES), jnp.int32), jax.ShapeDtypeStruct((n, LANES), F32),
                      jax.ShapeDtypeStruct((8, LANES), F32)]
        out_specs += [row(LANES), row(LANES), pl.BlockSpec((8, LANES), lambda r: (0, 0))]
        scratch = [pltpu.VMEM((8, LANES), F32)]
    return pl.pallas_call(
        functools.partial(_mix_out_body, router is not None, tm),
        grid=(n // tm,),
        in_specs=in_specs,
        out_specs=out_specs,
        out_shape=out_shape,
        scratch_shapes=scratch,
        compiler_params=_cparams(("arbitrary",)),
        name="mix_out_router" if router is not None else "mix_out",
    )(*args)


def _ffn_dense_body(h_ref, wg_ref, wu_ref, wd_ref, x_ref, gate_ref, g_ref, o_ref, acc_ref):
    j = pl.program_id(1)

    @pl.when(j == 0)
    def _():
        acc_ref[...] = jnp.zeros(acc_ref.shape, F32)

    h = h_ref[...]
    a = _nn(h, wg_ref[...].astype(BF16))
    u = _nn(h, wu_ref[...].astype(BF16))
    acc_ref[...] += _nn((a * _sigmoid(a) * u).astype(BF16), wd_ref[...].astype(BF16))

    @pl.when(j == pl.num_programs(1) - 1)
    def _():
        o_ref[...] = x_ref[...] + gate_ref[0] * (_rms(acc_ref[...], D_MODEL) * g_ref[...])


def _ffn_dense(h, wg, wu, wd, x2, gate, g, bsz, seq):
    n = bsz * seq
    d = D_MODEL
    tm = min(1024, seq)
    tps = seq // tm
    tf = 512
    return pl.pallas_call(
        _ffn_dense_body,
        grid=(n // tm, D_FF // tf),
        in_specs=[pl.BlockSpec((tm, d), lambda i, j: (i, 0)),
                  pl.BlockSpec((d, tf), lambda i, j: (0, j)),
                  pl.BlockSpec((d, tf), lambda i, j: (0, j)),
                  pl.BlockSpec((tf, d), lambda i, j: (j, 0)),
                  pl.BlockSpec((tm, d), lambda i, j: (i, 0)),
                  pl.BlockSpec((1, 1, d), lambda i, j: (i // tps, 0, 0)),
                  pl.BlockSpec(g.shape, lambda i, j: (0, 0))],
        out_specs=pl.BlockSpec((tm, d), lambda i, j: (i, 0)),
        out_shape=jax.ShapeDtypeStruct((n, d), F32),
        scratch_shapes=[pltpu.VMEM((tm, d), F32)],
        compiler_params=_cparams(("arbitrary", "arbitrary")),
        name="ffn_dense",
    )(h, wg, wu, wd, x2, gate, g)


def _ffn_expert_body(blk_e_ref, nvalid_ref, x_ref, wg_ref, wu_ref, wd_ref, o_ref, acc_ref):
    i = pl.program_id(0)
    j = pl.program_id(1)
    valid = i < nvalid_ref[0]

    @pl.when(j == 0)
    def _():
        acc_ref[...] = jnp.zeros(acc_ref.shape, F32)

    @pl.when(valid)
    def _():
        xb = x_ref[...]
        a = _nn(xb, wg_ref[0].astype(BF16))
        u = _nn(xb, wu_ref[0].astype(BF16))
        acc_ref[...] += _nn((a * _sigmoid(a) * u).astype(BF16), wd_ref[0].astype(BF16))

    @pl.when(j == pl.num_programs(1) - 1)
    def _():
        o_ref[...] = acc_ref[...]


def _ffn_expert(xb, wg, wu, wd, blk_e, nvalid, tmb):
    p, d = xb.shape
    nb = p // tmb
    tf = 512
    nf = D_FF // tf

    def jeff(i, j, nv):
        return jnp.where(i < nv[0], j, nf - 1)

    grid_spec = pltpu.PrefetchScalarGridSpec(
        num_scalar_prefetch=2,
        grid=(nb, nf),
        in_specs=[pl.BlockSpec((tmb, d), lambda i, j, be, nv: (i, 0)),
                  pl.BlockSpec((1, d, tf), lambda i, j, be, nv: (be[i], 0, jeff(i, j, nv))),
                  pl.BlockSpec((1, d, tf), lambda i, j, be, nv: (be[i], 0, jeff(i, j, nv))),
                  pl.BlockSpec((1, tf, d), lambda i, j, be, nv: (be[i], jeff(i, j, nv), 0))],
        out_specs=pl.BlockSpec((tmb, d), lambda i, j, be, nv: (i, 0)),
        scratch_shapes=[pltpu.VMEM((tmb, d), F32)],
    )
    return pl.pallas_call(
        _ffn_expert_body,
        grid_spec=grid_spec,
        out_shape=jax.ShapeDtypeStruct((p, d), F32),
        compiler_params=_cparams(("arbitrary", "arbitrary")),
        name="ffn_expert",
    )(blk_e, nvalid, xb, wg, wu, wd)


def _moe_combine_body(y1_ref, y2_ref, rg_ref, x_ref, gate_ref, g_ref, o_ref):
    rg = rg_ref[...]
    y = rg[:, 0:1] * y1_ref[...] + rg[:, 1:2] * y2_ref[...]
    o_ref[...] = x_ref[...] + gate_ref[0] * (_rms(y, D_MODEL) * g_ref[...])


def _moe_combine(y1, y2, rg, x2, gate, g, bsz, seq):
    n = bsz * seq
    d = D_MODEL
    tm = min(512, seq)
    tps = seq // tm
    row = lambda w: pl.BlockSpec((tm, w), lambda r: (r, 0))
    return pl.pallas_call(
        _moe_combine_body,
        grid=(n // tm,),
        in_specs=[row(d), row(d), row(LANES), row(d), pl.BlockSpec((1, 1, d), lambda r: (r // tps, 0, 0)),
                  pl.BlockSpec(g.shape, lambda r: (0, 0))],
        out_specs=row(d),
        out_shape=jax.ShapeDtypeStruct((n, d), F32),
        compiler_params=_cparams(("arbitrary",)),
        name="moe_combine",
    )(y1, y2, rg, x2, gate, g)


def _moe_ffn(h, ri, rg, counts, wg, wu, wd, x2, gate, g, bsz, seq):
    n = bsz * seq
    tmb = min(1024, n // 8)
    cnt = counts[0, :N_EXPERTS].astype(jnp.int32)
    padded = ((cnt + tmb - 1) // tmb) * tmb
    pend = jnp.cumsum(padded)
    pstart = pend - padded
    dest1 = pstart[ri[:, 0]] + ri[:, 2]
    dest2 = pstart[ri[:, 1]] + ri[:, 3]
    nb = (2 * n) // tmb + N_EXPERTS
    p = nb * tmb
    tok = jnp.arange(n, dtype=jnp.int32)
    buf_tok = jnp.zeros((p,), jnp.int32).at[dest1].set(tok).at[dest2].set(tok)
    nvalid = (pend[-1] // tmb).astype(jnp.int32).reshape(1)
    blk = jnp.minimum(jnp.arange(nb, dtype=jnp.int32), nvalid[0] - 1) * tmb
    blk_e = jnp.minimum(jnp.sum(blk[:, None] >= pend[None, :], axis=1), N_EXPERTS - 1).astype(jnp.int32)
    xb = jnp.take(h, buf_tok, axis=0)
    yb = _ffn_expert(xb, wg, wu, wd, blk_e, nvalid, tmb)
    y1 = jnp.take(yb, dest1, axis=0)
    y2 = jnp.take(yb, dest2, axis=0)
    return _moe_combine(y1, y2, rg, x2, gate, g, bsz, seq)


def _rot_cols(w):
    half = C_ROPE // 2
    return jnp.concatenate([-w[:, half:], w[:, :half]], axis=1)


def _prep_w_in(w_in):
    d = w_in.shape[0]
    z = lambda k: jnp.zeros((d, k), w_in.dtype)
    gates = w_in[:, 2304:2316]
    cq = w_in[:, 2316:2508]
    ckv = w_in[:, 2508:2636]
    kr = w_in[:, 2636:2668]
    cols = [w_in[:, :2304], cq, z(64), ckv,
            gates, z(C_NOPE - 12), kr, z(LANES - C_NOPE - C_ROPE),
            z(C_NOPE), _rot_cols(kr), z(LANES - C_NOPE - C_ROPE)]
    out = jnp.concatenate(cols, axis=1).astype(BF16)
    assert out.shape[1] == IN_COLS
    return out


def _prep_w_uq(w_uq):
    w = w_uq.reshape(C_Q_LORA, C_HEADS, C_NOPE + C_ROPE)
    nope, rp = w[..., :C_NOPE], w[..., C_NOPE:]
    pad = jnp.zeros((C_Q_LORA, C_HEADS, MLA_SLAB - C_NOPE - C_ROPE), w.dtype)
    rot = jnp.concatenate([-rp[..., C_ROPE // 2:], rp[..., :C_ROPE // 2]], axis=-1)
    wq = jnp.concatenate([nope, rp, pad], axis=-1).reshape(C_Q_LORA, MLA_QK_W)
    wqr = jnp.concatenate([jnp.zeros_like(nope), rot, pad], axis=-1).reshape(C_Q_LORA, MLA_QK_W)
    rowpad = jnp.zeros((256 - C_Q_LORA, MLA_QK_W), w.dtype)
    return (jnp.concatenate([wq, rowpad], axis=0).astype(BF16), jnp.concatenate([wqr, rowpad], axis=0).astype(BF16))


def _prep_w_ukv(w_ukv):
    w = w_ukv.reshape(C_KV_LORA, C_HEADS, C_NOPE + C_V_DIM)
    kn = jnp.concatenate([w[..., :C_NOPE], jnp.zeros((C_KV_LORA, C_HEADS, MLA_SLAB - C_NOPE), w.dtype)], axis=-1)
    v = w[..., C_NOPE:]
    return jnp.concatenate([kn.reshape(C_KV_LORA, MLA_QK_W), v.reshape(C_KV_LORA, MLA_V_W)], axis=1).astype(BF16)


def _rope_tables(positions):
    freqs = ROPE_THETA ** (-jnp.arange(0, C_ROPE, 2, dtype=F32) / C_ROPE)
    ang = positions.astype(F32).reshape(-1, 1) * freqs
    n = ang.shape[0]
    cos, sin = jnp.cos(ang), jnp.sin(ang)
    ones = lambda k: jnp.ones((n, k), F32)
    zeros = lambda k: jnp.zeros((n, k), F32)
    tail = MLA_SLAB - C_NOPE - C_ROPE
    return (jnp.concatenate([ones(C_NOPE), cos, cos, ones(tail)], axis=1),
            jnp.concatenate([zeros(C_NOPE), sin, sin, zeros(tail)], axis=1))


def kernel(x, c, positions, w_ada, b_ada, g_norm, w_in, w_out, lam_q1, lam_k1, lam_q2, lam_k2, g_diff, w_conv, b_conv, b_igate, b_fgate, g_mlstm, g_q_lat, g_kv_lat, w_uq, w_ukv, w_gate_d, w_up_d, w_down_d, w_router, b_router, w_gate_e, w_up_e, w_down_e):
    bsz, seq, d = x.shape
    n = bsz * seq
    nc = seq // CHUNK
    depth = w_in.shape[0]
    mods = _ada_all(c, w_ada, b_ada)
    cos_t, sin_t = _rope_tables(positions)
    x2 = x.reshape(n, d)
    row1 = lambda v: v.reshape(1, -1).astype(F32)

    for l in range(depth):
        mod = lambda k, j: mods[2 * l + k, j].reshape(bsz, 1, d)
        wq, wqr = _prep_w_uq(w_uq[l])
        gq = jnp.concatenate([g_q_lat[l], jnp.zeros((256 - C_Q_LORA,), F32)]).reshape(1, 256)
        (qd, kd, vd, qm, km, vm, om, gt, qc, kc, vc) = _mix_in(
            x2, mod(0, 0), mod(0, 1), row1(g_norm[l, 0]), _prep_w_in(w_in[l]), w_conv[l], row1(b_conv[l]),
            gq, row1(g_kv_lat[l]), wq, wqr, _prep_w_ukv(w_ukv[l]), cos_t, sin_t, bsz, seq)

        lam_init = 0.8 - 0.6 * math.exp(-0.3 * l)
        lamv = jnp.zeros((8, LANES), F32).at[0:4, 0:A_QK_DIM].set(jnp.stack([lam_q1[l], lam_k1[l], lam_q2[l], lam_k2[l]]))
        out_a = _diff_attn(qd, kd, vd, lamv, row1(jnp.tile(g_diff[l], A_HEADS)), lam_init, bsz, seq)

        z2 = jnp.zeros((bsz, 2, seq), F32)
        g16 = jnp.concatenate([gt[:, 0:6], z2, gt[:, 6:12], z2], axis=1)
        gates = jnp.pad(g16.reshape(bsz, 16, nc, CHUNK).transpose(0, 2, 1, 3), ((0, 0), (0, 0), (0, 0), (0, LANES - CHUNK)))
        zb = jnp.zeros((2,), F32)
        gbias = jnp.broadcast_to(jnp.concatenate([b_igate[l], zb, b_fgate[l], zb])[:, None], (16, LANES)).astype(F32)
        out_b = _mlstm(qm, km, vm, om, gates, gbias, row1(g_mlstm[l]), bsz, seq)

        out_c = _mla_attn(qc, kc, vc, bsz, seq)

        moe = (l % 2 == 1)
        router = None
        if moe:
            w_hi = w_router[l // 2].astype(BF16)
            w_lo = (w_router[l // 2] - w_hi.astype(F32)).astype(BF16)
            wr = jnp.pad(jnp.concatenate([w_hi, w_lo], axis=1), ((0, 0), (0, LANES - 2 * N_EXPERTS)))
            br = jnp.pad(b_router[l // 2], (0, LANES - N_EXPERTS)).reshape(1, LANES)
            router = (wr, br)
        res = _mix_out(out_a, out_b, out_c, w_out[l].astype(BF16), x2, mod(0, 2), row1(g_norm[l, 1]),
                       mod(1, 0), mod(1, 1), row1(g_norm[l, 2]), router, bsz, seq)
        if moe:
            x2, h2, ri, rg, counts = res
            x2 = _moe_ffn(h2, ri, rg, counts, w_gate_e[l // 2], w_up_e[l // 2], w_down_e[l // 2],
                          x2, mod(1, 2), row1(g_norm[l, 3]), bsz, seq)
        else:
            x2, h2 = res
            x2 = _ffn_dense(h2, w_gate_d[l // 2], w_up_d[l // 2], w_down_d[l // 2],
                            x2, mod(1, 2), row1(g_norm[l, 3]), bsz, seq)
    return x2.reshape(bsz, seq, d)
```

```python
import functools
import math

import jax
import jax.numpy as jnp
import numpy as np
from jax import lax
from jax.experimental import pallas as pl
from jax.experimental.pallas import tpu as pltpu

F32 = jnp.float32
BF16 = jnp.bfloat16
HIGHEST = lax.Precision.HIGHEST

D_MODEL = 1024
DEPTH = 2
CHUNK = 64
EPS = 1e-6
A_HEADS, A_QK_DIM, A_V_DIM = 4, 32, 64
B_HEADS, B_DIM, CONV_WIDTH = 6, 64, 4
C_HEADS, C_Q_LORA, C_KV_LORA, C_NOPE, C_ROPE, C_V_DIM = 6, 192, 128, 64, 32, 64
ROPE_THETA = 10000.0
D_FF = 3584
N_EXPERTS = 8
LOG2E = 1.4426950408889634
NEG_BIG = -1e30

LANES = 128
DIFF_W = 2 * A_HEADS * A_QK_DIM
MLSTM_W = B_HEADS * B_DIM
MLA_SLAB = 128
MLA_QK_W = C_HEADS * MLA_SLAB
MLA_V_W = C_HEADS * C_V_DIM

COL_DIFF = 0
COL_MLSTM = 768
COL_CQ = 2304
COL_CKV = 2560
COL_X = 2688
COL_Y = 2816
IN_COLS = 2944

VMEM_LIMIT = 56 * 1024 * 1024


def _cparams(sem):
    return pltpu.CompilerParams(dimension_semantics=sem, vmem_limit_bytes=VMEM_LIMIT)


def _nt(a, b):
    return lax.dot_general(a, b, (((1,), (1,)), ((), ())), preferred_element_type=F32)


def _nn(a, b):
    return jnp.dot(a, b, preferred_element_type=F32)


def _sigmoid(v):
    return 1.0 / (1.0 + jnp.exp(-v))


def _rms(v, n):
    return v * lax.rsqrt(jnp.sum(v * v, axis=-1, keepdims=True) * (1.0 / n) + EPS)


def _ada_body(c_ref, w_ref, b_ref, o_ref):
    c = c_ref[...]
    sc = c * _sigmoid(c)
    o_ref[0, 0] = jnp.dot(sc, w_ref[0], precision=HIGHEST, preferred_element_type=F32) + b_ref[0]


def _ada_all(c, w_ada, b_ada):
    bsz, d = c.shape
    n = w_ada.shape[0] * w_ada.shape[1]
    w = w_ada.reshape(n, d, 3 * d)
    b = b_ada.reshape(n, 1, 3 * d)
    return pl.pallas_call(
        _ada_body,
        grid=(n, 3),
        in_specs=[pl.BlockSpec((bsz, d), lambda i, j: (0, 0)),
                  pl.BlockSpec((1, d, d), lambda i, j: (i, 0, j)),
                  pl.BlockSpec((1, 1, d), lambda i, j: (i, 0, j))],
        out_specs=pl.BlockSpec((1, 1, bsz, d), lambda i, j: (i, j, 0, 0)),
        out_shape=jax.ShapeDtypeStruct((n, 3, bsz, d), F32),
        compiler_params=_cparams(("arbitrary", "arbitrary")),
        name="ada_mod",
    )(c, w, b)


def _mix_in_body(tiles_per_seq, tm,
                 x_ref, shift_ref, scale_ref, g_ref, win_ref, wconv_ref, bconv_ref, gq_ref, gkv_ref,
                 wq_ref, wqr_ref, wkv_ref, cos_ref, sin_ref,
                 qd_ref, kd_ref, vd_ref, qm_ref, km_ref, vm_ref, om_ref, gt_ref, qc_ref, kc_ref, vc_ref,
                 carry_ref):
    r = pl.program_id(0)
    x = x_ref[...]
    h = _rms(x, D_MODEL) * g_ref[...]
    h = h * (1.0 + scale_ref[0]) + shift_ref[0]
    p = _nn(h.astype(BF16), win_ref[...])

    qd_ref[...] = (p[:, 0:256] * (A_QK_DIM ** -0.5 * LOG2E)).astype(BF16)
    kd_ref[...] = p[:, 256:512].astype(BF16)
    vd_ref[...] = p[:, 512:768].astype(BF16)

    @pl.when(r % tiles_per_seq == 0)
    def _():
        carry_ref[0:8, :] = jnp.zeros((8, 2 * MLSTM_W), F32)

    carry_ref[8:8 + tm, :] = p[:, COL_MLSTM:COL_MLSTM + 2 * MLSTM_W]
    w = wconv_ref[...]
    conv = bconv_ref[...] + carry_ref[8:8 + tm, :] * w[3:4, :]
    for j in range(CONV_WIDTH - 1):
        conv = conv + carry_ref[5 + j:5 + j + tm, :] * w[j:j + 1, :]
    carry_ref[0:8, :] = carry_ref[tm:tm + 8, :]
    qk = conv * _sigmoid(conv)
    qm_ref[...] = qk[:, 0:MLSTM_W].astype(BF16)
    km_ref[...] = (qk[:, MLSTM_W:2 * MLSTM_W] * (B_DIM ** -0.5)).astype(BF16)
    vm_ref[...] = p[:, COL_MLSTM + 2 * MLSTM_W:COL_MLSTM + 3 * MLSTM_W].astype(BF16)
    om_ref[...] = p[:, COL_MLSTM + 3 * MLSTM_W:COL_MLSTM + 4 * MLSTM_W].astype(BF16)

    xg = p[:, COL_X:COL_X + LANES]
    gt_ref[0] = xg.T[0:16, :]

    cq = p[:, COL_CQ:COL_CQ + 256]
    cqn = (cq * lax.rsqrt(jnp.sum(cq * cq, axis=-1, keepdims=True) * (1.0 / C_Q_LORA) + EPS) * gq_ref[...]).astype(BF16)
    ckv = p[:, COL_CKV:COL_CKV + C_KV_LORA]
    ckvn = (_rms(ckv, C_KV_LORA) * gkv_ref[...]).astype(BF16)
    cos_t = cos_ref[...]
    sin_t = sin_ref[...]
    cos6 = jnp.concatenate([cos_t] * C_HEADS, axis=1)
    sin6 = jnp.concatenate([sin_t] * C_HEADS, axis=1)
    qn = _nn(cqn, wq_ref[...])
    qr = _nn(cqn, wqr_ref[...])
    qc_ref[...] = ((qn * cos6 + qr * sin6) * ((C_NOPE + C_ROPE) ** -0.5 * LOG2E)).astype(BF16)
    kvp = _nn(ckvn, wkv_ref[...])
    lane = lax.broadcasted_iota(jnp.int32, (1, LANES), 1)
    rope_lanes = (lane >= C_NOPE) & (lane < C_NOPE + C_ROPE)
    yg = p[:, COL_Y:COL_Y + LANES]
    krope = jnp.where(rope_lanes, xg * cos_t + yg * sin_t, 0.0)
    kc_ref[...] = (kvp[:, 0:MLA_QK_W] + jnp.concatenate([krope] * C_HEADS, axis=1)).astype(BF16)
    vc_ref[...] = kvp[:, MLA_QK_W:MLA_QK_W + MLA_V_W].astype(BF16)


def _mix_in(x2, shift, scale, g, win, wconv, bconv, gq, gkv, wq, wqr, wkv, cos_t, sin_t, bsz, seq):
    n = bsz * seq
    tm = min(512, seq)
    tps = seq // tm
    d = D_MODEL
    row = lambda w: pl.BlockSpec((tm, w), lambda r: (r, 0))
    full = lambda a: pl.BlockSpec(a.shape, lambda r: (0,) * a.ndim)
    mod = pl.BlockSpec((1, 1, d), lambda r: (r // tps, 0, 0))
    out_widths = [DIFF_W, DIFF_W, DIFF_W, MLSTM_W, MLSTM_W, MLSTM_W, MLSTM_W]
    out_shape = [jax.ShapeDtypeStruct((n, w), BF16) for w in out_widths]
    out_specs = [row(w) for w in out_widths]
    out_shape.append(jax.ShapeDtypeStruct((bsz, 16, seq), F32))
    out_specs.append(pl.BlockSpec((1, 16, tm), lambda r: (r // tps, 0, r % tps)))
    for w in (MLA_QK_W, MLA_QK_W, MLA_V_W):
        out_shape.append(jax.ShapeDtypeStruct((n, w), BF16))
        out_specs.append(row(w))
    return pl.pallas_call(
        functools.partial(_mix_in_body, tps, tm),
        grid=(n // tm,),
        in_specs=[row(d), mod, mod, full(g), full(win), full(wconv), full(bconv), full(gq), full(gkv),
                  full(wq), full(wqr), full(wkv), row(LANES), row(LANES)],
        out_specs=out_specs,
        out_shape=out_shape,
        scratch_shapes=[pltpu.VMEM((tm + 8, 2 * MLSTM_W), F32)],
        compiler_params=_cparams(("arbitrary",)),
        name="mix_in",
    )(x2, shift, scale, g, win, wconv, bconv, gq, gkv, wq, wqr, wkv, cos_t, sin_t)


def _chunk_mask(tq, tk):
    ri = lax.broadcasted_iota(jnp.int32, (tq, tk), 0)
    ci = lax.broadcasted_iota(jnp.int32, (tq, tk), 1)
    return (ci >> 6) <= (ri >> 6)


def _two_pass_softmax(i, nq, tk, score_fn, pv_fn, mask, s_ref, p_ref, mx_ref, l_ref):
    rows = mx_ref.shape[0]
    tq = mask.shape[0]
    mx_ref[...] = jnp.full(mx_ref.shape, NEG_BIG, F32)

    def pass1(j, masked):
        s = score_fn(j)
        if masked:
            s = jnp.where(mask, s.reshape(rows // tq, tq, tk), NEG_BIG).reshape(rows, tk)
        s_ref[j] = s
        mx_ref[...] = jnp.maximum(mx_ref[...], jnp.maximum(s[:, 0:LANES], s[:, LANES:2 * LANES]))

    def body1(j, c):
        pass1(j, False)
        return c

    lax.fori_loop(0, i, body1, 0)
    pass1(i, True)
    mx_ref[...] = jnp.broadcast_to(jnp.max(mx_ref[...], axis=1, keepdims=True), mx_ref.shape)

    rb = 128

    def pass2_pv(n):
        for j in range(n):
            for r0 in range(0, rows, rb):
                s = s_ref[j, r0:r0 + rb, :]
                m = mx_ref[r0:r0 + rb, :]
                p0 = jnp.exp2(s[:, 0:LANES] - m)
                p1 = jnp.exp2(s[:, LANES:2 * LANES] - m)
                l_ref[r0:r0 + rb, :] = (p0 + p1) if j == 0 else (l_ref[r0:r0 + rb, :] + (p0 + p1))
                p_ref[j, r0:r0 + rb, :] = jnp.concatenate([p0, p1], axis=1).astype(BF16)
        pv_fn(n)

    for n in range(1, nq + 1):
        @pl.when(i == n - 1)
        def _():
            pass2_pv(n)
    return jnp.sum(l_ref[...], axis=1, keepdims=True)


def _diff_attn_body(tq, nq, lam_init, q_ref, k_ref, v_ref, lam_ref, g_ref, o_ref,
                    qs_ref, s_ref, p_ref, mx_ref, l_ref, acc_ref):
    i = pl.program_id(1)
    tk = tq
    nslab = 2 * A_HEADS
    q = q_ref[...]
    lane = lax.broadcasted_iota(jnp.int32, (1, DIFF_W), 1)
    zero = jnp.zeros_like(q)
    for s in range(nslab):
        qs_ref[s * tq:(s + 1) * tq, :] = jnp.where((lane >= A_QK_DIM * s) & (lane < A_QK_DIM * (s + 1)), q, zero)
    mask = _chunk_mask(tq, tk)

    def score(j):
        return _nt(qs_ref[...], k_ref[pl.ds(pl.multiple_of(j * tk, tk), tk), :])

    def pv(n):
        acc_ref[...] = sum(_nn(p_ref[j], v_ref[j * tk:(j + 1) * tk, :]) for j in range(n))

    l = _two_pass_softmax(i, nq, tk, score, pv, mask, s_ref, p_ref, mx_ref, l_ref)

    lv = lam_ref[...]
    lam = (jnp.exp(jnp.sum(lv[0:1] * lv[1:2], axis=1, keepdims=True))
           - jnp.exp(jnp.sum(lv[2:3] * lv[3:4], axis=1, keepdims=True)) + lam_init)
    o = acc_ref[...] / l
    out = jnp.zeros((tq, DIFF_W), F32)
    for hd in range(A_HEADS):
        sel = (lane >= A_V_DIM * hd) & (lane < A_V_DIM * (hd + 1))
        oh = o[hd * tq:(hd + 1) * tq] - lam * o[(A_HEADS + hd) * tq:(A_HEADS + hd + 1) * tq]
        ms = jnp.sum(jnp.where(sel, oh * oh, 0.0), axis=1, keepdims=True) * (1.0 / A_V_DIM)
        out = out + jnp.where(sel, oh * lax.rsqrt(ms + EPS), 0.0)
    o_ref[...] = (out * g_ref[...] * (1.0 - lam_init)).astype(BF16)


def _diff_attn(qd, kd, vd, lamv, g4, lam_init, bsz, seq):
    tq = min(256, seq)
    nq = seq // tq
    nslab = 2 * A_HEADS
    rows = nslab * tq
    return pl.pallas_call(
        functools.partial(_diff_attn_body, tq, nq, lam_init),
        grid=(bsz, nq),
        in_specs=[pl.BlockSpec((tq, DIFF_W), lambda b, i: (b * nq + i, 0)),
                  pl.BlockSpec((seq, DIFF_W), lambda b, i: (b, 0)),
                  pl.BlockSpec((seq, DIFF_W), lambda b, i: (b, 0)),
                  pl.BlockSpec(lamv.shape, lambda b, i: (0, 0)),
                  pl.BlockSpec(g4.shape, lambda b, i: (0, 0))],
        out_specs=pl.BlockSpec((tq, DIFF_W), lambda b, i: (b * nq + i, 0)),
        out_shape=jax.ShapeDtypeStruct((bsz * seq, DIFF_W), BF16),
        scratch_shapes=[pltpu.VMEM((rows, DIFF_W), BF16), pltpu.VMEM((nq, rows, tq), F32),
                        pltpu.VMEM((nq, rows, tq), BF16), pltpu.VMEM((rows, LANES), F32),
                        pltpu.VMEM((rows, LANES), F32), pltpu.VMEM((rows, DIFF_W), F32)],
        compiler_params=_cparams(("arbitrary", "arbitrary")),
        name="diff_attn",
    )(qd, kd, vd, lamv, g4)


def _mla_attn_body(tq, nq, q_ref, k_ref, v_ref, o_ref, s_ref, p_ref, mx_ref, l_ref, acc1_ref, acc2_ref):
    i = pl.program_id(1)
    tk = tq
    mask = _chunk_mask(tq, tk)
    g1 = 4

    def score(j):
        kt = k_ref[pl.ds(pl.multiple_of(j * tk, tk), tk), :]
        return jnp.concatenate([_nt(q_ref[:, MLA_SLAB * hd:MLA_SLAB * (hd + 1)], kt[:, MLA_SLAB * hd:MLA_SLAB * (hd + 1)])
                                for hd in range(C_HEADS)], axis=0)

    def pv(n):
        acc1_ref[...] = sum(_nn(p_ref[j, 0:g1 * tq, :], v_ref[j * tk:(j + 1) * tk, 0:256]) for j in range(n))
        acc2_ref[...] = sum(_nn(p_ref[j, g1 * tq:, :], v_ref[j * tk:(j + 1) * tk, 256:MLA_V_W]) for j in range(n))

    l = _two_pass_softmax(i, nq, tk, score, pv, mask, s_ref, p_ref, mx_ref, l_ref)

    linv = 1.0 / l
    o1 = acc1_ref[...] * linv[0:g1 * tq]
    o2 = acc2_ref[...] * linv[g1 * tq:]
    lane1 = lax.broadcasted_iota(jnp.int32, (1, 256), 1)
    lane2 = lax.broadcasted_iota(jnp.int32, (1, 128), 1)
    out1 = jnp.zeros((tq, 256), F32)
    for hd in range(g1):
        out1 = out1 + jnp.where((lane1 >= C_V_DIM * hd) & (lane1 < C_V_DIM * (hd + 1)), o1[hd * tq:(hd + 1) * tq], 0.0)
    out2 = jnp.zeros((tq, 128), F32)
    for hd in range(C_HEADS - g1):
        out2 = out2 + jnp.where((lane2 >= C_V_DIM * hd) & (lane2 < C_V_DIM * (hd + 1)), o2[hd * tq:(hd + 1) * tq], 0.0)
    o_ref[...] = jnp.concatenate([out1, out2], axis=1).astype(BF16)


def _mla_attn(qc, kc, vc, bsz, seq):
    tq = min(256, seq)
    nq = seq // tq
    return pl.pallas_call(
        functools.partial(_mla_attn_body, tq, nq),
        grid=(bsz, nq),
        in_specs=[pl.BlockSpec((tq, MLA_QK_W), lambda b, i: (b * nq + i, 0)),
                  pl.BlockSpec((seq, MLA_QK_W), lambda b, i: (b, 0)),
                  pl.BlockSpec((seq, MLA_V_W), lambda b, i: (b, 0))],
        out_specs=pl.BlockSpec((tq, MLA_V_W), lambda b, i: (b * nq + i, 0)),
        out_shape=jax.ShapeDtypeStruct((bsz * seq, MLA_V_W), BF16),
        scratch_shapes=[pltpu.VMEM((nq, C_HEADS * tq, tq), F32), pltpu.VMEM((nq, C_HEADS * tq, tq), BF16),
                        pltpu.VMEM((C_HEADS * tq, LANES), F32), pltpu.VMEM((C_HEADS * tq, LANES), F32),
                        pltpu.VMEM((4 * tq, 256), F32), pltpu.VMEM((2 * tq, 128), F32)],
        compiler_params=_cparams(("arbitrary", "arbitrary")),
        name="mla_attn",
    )(qc, kc, vc)


def _mlstm_body(nc, q_ref, k_ref, v_ref, og_ref, gates_ref, gbias_ref, gout_ref, o_ref,
                ct_ref, m_ref, a_ref, b_ref, al_ref, bl_ref):
    ct_ref[...] = jnp.zeros(ct_ref.shape, F32)
    m_ref[...] = jnp.zeros(m_ref.shape, F32)
    lane = lax.broadcasted_iota(jnp.int32, (1, LANES), 1)
    ti = lax.broadcasted_iota(jnp.int32, (CHUNK, CHUNK), 0)
    si = lax.broadcasted_iota(jnp.int32, (CHUNK, CHUNK), 1)
    tril = si <= ti
    eye = si == ti
    eye128 = (lax.broadcasted_iota(jnp.int32, (LANES, LANES), 0)
              == lax.broadcasted_iota(jnp.int32, (LANES, LANES), 1)).astype(BF16)
    gbias = gbias_ref[...]
    gout = gout_ref[...]

    g = gates_ref[0]
    ig = (g[:, 0:8, :] + gbias[0:8]).reshape(nc * 8, LANES)
    fg = (g[:, 8:16, :] + gbias[8:16]).reshape(nc * 8, LANES)
    lf = jnp.minimum(fg, 0.0) - jnp.log(1.0 + jnp.exp(-jnp.abs(fg)))
    b_all = jnp.where(lane < CHUNK, lf, 0.0)
    for sh in (1, 2, 4, 8, 16, 32):
        b_all = b_all + jnp.where(lane >= sh, pltpu.roll(b_all, sh, 1), 0.0)
    a_all = ig - b_all
    a_ref[...] = a_all.reshape(nc, 8, LANES)
    b_ref[...] = b_all.reshape(nc, 8, LANES)
    al_ref[...] = jnp.broadcast_to(jnp.max(jnp.where(lane < CHUNK, a_all, NEG_BIG), axis=1, keepdims=True),
                                   (nc * 8, LANES)).reshape(nc, 8, LANES)
    bl_ref[...] = jnp.broadcast_to(jnp.sum(jnp.where(lane == CHUNK - 1, b_all, 0.0), axis=1, keepdims=True),
                                   (nc * 8, LANES)).reshape(nc, 8, LANES)

    def chunk(c, carry):
        rows = pl.ds(pl.multiple_of(c * CHUNK, CHUNK), CHUNK)
        a = a_ref[c]
        b = b_ref[c]
        b_last = bl_ref[c]

        for pr in range(B_HEADS // 2):
            cols = slice(LANES * pr, LANES * (pr + 1))
            qw = q_ref[rows, cols]
            kw = k_ref[rows, cols]
            vw = v_ref[rows, cols]
            kt = _nt(eye128, kw)
            ctp = ct_ref[pr]
            ctb = ctp.astype(BF16)
            outs = []
            for hh in range(2):
                hd = 2 * pr + hh
                half = (lane < CHUNK) if hh == 0 else (lane >= CHUNK)
                onecol = (lane == CHUNK) if hh == 0 else (lane == 0)
                qmk = jnp.where(half, qw, jnp.zeros_like(qw))
                vaug = jnp.where(half, vw, jnp.where(onecol, 1.0, 0.0).astype(BF16))
                a_h = a[hd:hd + 1, 0:CHUNK]
                b_h = b[hd:hd + 1, 0:CHUNK]
                full = (CHUNK, LANES)
                a_col = jnp.broadcast_to(jnp.max(jnp.where(tril, a_h, NEG_BIG), axis=1, keepdims=True), full)
                b_col = jnp.broadcast_to(jnp.sum(jnp.where(eye, b_h, 0.0), axis=1, keepdims=True), full)
                w0 = jnp.where(tril, jnp.exp(a_h - a_col[:, 0:CHUNK]), 0.0)
                sc0 = (_nt(qmk, kw) * w0).astype(BF16)
                x0 = _nn(sc0, vaug)
                xi = _nn(qmk, ctb)
                m = m_ref[hd:hd + 1, :]
                m_col = jnp.maximum(a_col, m)
                xs = jnp.exp(a_col - m_col) * x0 + jnp.exp(m - m_col) * xi
                den = jnp.broadcast_to(jnp.sum(jnp.where(onecol, xs, 0.0), axis=1, keepdims=True), full)
                hv = xs / jnp.maximum(jnp.abs(den), jnp.exp(-b_col - m_col))
                ms = jnp.broadcast_to(jnp.sum(jnp.where(half, hv * hv, 0.0), axis=1, keepdims=True), full)
                outs.append(hv * lax.rsqrt(ms * (1.0 / B_DIM) + EPS))
                al = al_ref[c][hd:hd + 1, :]
                m_l = jnp.maximum(al, m)
                kwt = (kt[CHUNK * hh:CHUNK * (hh + 1), :] * jnp.exp(a_h - al[:, 0:CHUNK])).astype(BF16)
                dct = _nn(kwt, vaug)
                ct_ref[pr, CHUNK * hh:CHUNK * (hh + 1), :] = (jnp.exp(m - m_l) * ctp[CHUNK * hh:CHUNK * (hh + 1)]
                                                              + jnp.exp(al - m_l) * dct)
                m_ref[hd:hd + 1, :] = b_last[hd:hd + 1, :] + m_l
            hn = jnp.where(lane < CHUNK, outs[0], outs[1])
            og = og_ref[rows, cols].astype(F32)
            o_ref[rows, cols] = (hn * gout[:, cols] * _sigmoid(og)).astype(BF16)
        return carry

    lax.fori_loop(0, nc, chunk, 0, unroll=2)


def _mlstm(qm, km, vm, om, gates, gbias, gout, bsz, seq):
    nc = seq // CHUNK
    blk = pl.BlockSpec((seq, MLSTM_W), lambda b: (b, 0))
    return pl.pallas_call(
        functools.partial(_mlstm_body, nc),
        grid=(bsz,),
        in_specs=[blk, blk, blk, blk,
                  pl.BlockSpec((1, nc, 16, LANES), lambda b: (b, 0, 0, 0)),
                  pl.BlockSpec(gbias.shape, lambda b: (0, 0)),
                  pl.BlockSpec(gout.shape, lambda b: (0, 0))],
        out_specs=blk,
        out_shape=jax.ShapeDtypeStruct((bsz * seq, MLSTM_W), BF16),
        scratch_shapes=[pltpu.VMEM((B_HEADS // 2, LANES, LANES), F32), pltpu.VMEM((8, LANES), F32)]
        + [pltpu.VMEM((nc, 8, LANES), F32)] * 4,
        compiler_params=_cparams(("arbitrary",)),
        name="mlstm",
    )(qm, km, vm, om, gates, gbias, gout)


def _mix_out_body(with_router, tm,
                  oa_ref, ob_ref, oc_ref, wout_ref, x_ref, gate_ref, g1_ref, shift_ref, scale_ref, g2_ref, *rest):
    if with_router:
        wr_ref, br_ref, xo_ref, h_ref, ri_ref, rg_ref, cnt_ref, run_ref = rest
    else:
        xo_ref, h_ref = rest
    na = A_HEADS * A_V_DIM
    y = (_nn(oa_ref[...], wout_ref[0:na, :]) + _nn(ob_ref[...], wout_ref[na:na + MLSTM_W, :])
         + _nn(oc_ref[...], wout_ref[na + MLSTM_W:, :]))
    xn = x_ref[...] + gate_ref[0] * (_rms(y, D_MODEL) * g1_ref[...])
    xo_ref[...] = xn
    h = _rms(xn, D_MODEL) * g2_ref[...]
    h = h * (1.0 + scale_ref[0]) + shift_ref[0]
    h_ref[...] = h.astype(BF16)
    if with_router:
        @pl.when(pl.program_id(0) == 0)
        def _():
            run_ref[...] = jnp.zeros(run_ref.shape, F32)

        lane = lax.broadcasted_iota(jnp.int32, (1, LANES), 1)
        h_hi = h.astype(BF16)
        h_lo = (h - h_hi.astype(F32)).astype(BF16)
        part = _nn(h_hi, wr_ref[...]) + _nn(h_lo, wr_ref[...])
        logits = part + pltpu.roll(part, LANES - N_EXPERTS, 1) + br_ref[...]
        lg = jnp.where(lane < N_EXPERTS, logits, NEG_BIG)
        v1 = jnp.max(lg, axis=1, keepdims=True)
        i1 = jnp.min(jnp.where(lg == v1, lane, LANES), axis=1, keepdims=True)
        lg2 = jnp.where(lane == i1, NEG_BIG, lg)
        v2 = jnp.max(lg2, axis=1, keepdims=True)
        i2 = jnp.min(jnp.where(lg2 == v2, lane, LANES), axis=1, keepdims=True)
        gt1 = 1.0 / (1.0 + jnp.exp(v2 - v1))
        oh1 = lane == i1
        oh2 = lane == i2
        cnt = jnp.where(oh1 | oh2, 1.0, 0.0)
        ri_ = lax.broadcasted_iota(jnp.int32, (tm, tm), 0)
        ci_ = lax.broadcasted_iota(jnp.int32, (tm, tm), 1)
        lower = jnp.where(ci_ < ri_, 1.0, 0.0).astype(BF16)
        before = _nn(lower, cnt.astype(BF16)) + run_ref[0:1, :]
        r1 = jnp.sum(jnp.where(oh1, before, 0.0), axis=1, keepdims=True)
        r2 = jnp.sum(jnp.where(oh2, before, 0.0), axis=1, keepdims=True)
        run_ref[...] = run_ref[...] + jnp.sum(cnt, axis=0, keepdims=True)
        ri_ref[...] = jnp.where(lane == 0, i1, jnp.where(lane == 1, i2, jnp.where(
            lane == 2, r1.astype(jnp.int32), jnp.where(lane == 3, r2.astype(jnp.int32), 0))))
        rg_ref[...] = jnp.where(lane == 0, gt1, jnp.where(lane == 1, 1.0 - gt1, 0.0))
        cnt_ref[...] = run_ref[...]


def _mix_out(oa, ob, oc, wout, x2, gate, g1, shift, scale, g2, router, bsz, seq):
    n = bsz * seq
    d = D_MODEL
    tm = min(512, seq)
    tps = seq // tm
    row = lambda w: pl.BlockSpec((tm, w), lambda r: (r, 0))
    full = lambda a: pl.BlockSpec(a.shape, lambda r: (0,) * a.ndim)
    mod = pl.BlockSpec((1, 1, d), lambda r: (r // tps, 0, 0))
    in_specs = [row(oa.shape[1]), row(ob.shape[1]), row(oc.shape[1]), full(wout), row(d), mod, full(g1), mod, mod, full(g2)]
    args = [oa, ob, oc, wout, x2, gate, g1, shift, scale, g2]
    out_shape = [jax.ShapeDtypeStruct((n, d), F32), jax.ShapeDtypeStruct((n, d), BF16)]
    out_specs = [row(d), row(d)]
    scratch = []
    if router is not None:
        wr, br = router
        in_specs += [full(wr), full(br)]
        args += [wr, br]
        out_shape += [jax.ShapeDtypeStruct((n, LANES), jnp.int32), jax.ShapeDtypeStruct((n, LANES), F32),
                      jax.ShapeDtypeStruct((8, LANES), F32)]
        out_specs += [row(LANES), row(LANES), pl.BlockSpec((8, LANES), lambda r: (0, 0))]
        scratch = [pltpu.VMEM((8, LANES), F32)]
    return pl.pallas_call(
        functools.partial(_mix_out_body, router is not None, tm),
        grid=(n // tm,),
        in_specs=in_specs,
        out_specs=out_specs,
        out_shape=out_shape,
        scratch_shapes=scratch,
        compiler_params=_cparams(("arbitrary",)),
        name="mix_out_router" if router is not None else "mix_out",
    )(*args)


def _ffn_dense_body(h_ref, wg_ref, wu_ref, wd_ref, x_ref, gate_ref, g_ref, o_ref, acc_ref):
    j = pl.program_id(1)

    @pl.when(j == 0)
    def _():
        acc_ref[...] = jnp.zeros(acc_ref.shape, F32)

    h = h_ref[...]
    a = _nn(h, wg_ref[...].astype(BF16))
    u = _nn(h, wu_ref[...].astype(BF16))
    acc_ref[...] += _nn((a * _sigmoid(a) * u).astype(BF16), wd_ref[...].astype(BF16))

    @pl.when(j == pl.num_programs(1) - 1)
    def _():
        o_ref[...] = x_ref[...] + gate_ref[0] * (_rms(acc_ref[...], D_MODEL) * g_ref[...])


def _ffn_dense(h, wg, wu, wd, x2, gate, g, bsz, seq):
    n = bsz * seq
    d = D_MODEL
    tm = min(1024, seq)
    tps = seq // tm
    tf = 512
    return pl.pallas_call(
        _ffn_dense_body,
        grid=(n // tm, D_FF // tf),
        in_specs=[pl.BlockSpec((tm, d), lambda i, j: (i, 0)),
                  pl.BlockSpec((d, tf), lambda i, j: (0, j)),
                  pl.BlockSpec((d, tf), lambda i, j: (0, j)),
                  pl.BlockSpec((tf, d), lambda i, j: (j, 0)),
                  pl.BlockSpec((tm, d), lambda i, j: (i, 0)),
                  pl.BlockSpec((1, 1, d), lambda i, j: (i // tps, 0, 0)),
                  pl.BlockSpec(g.shape, lambda i, j: (0, 0))],
        out_specs=pl.BlockSpec((tm, d), lambda i, j: (i, 0)),
        out_shape=jax.ShapeDtypeStruct((n, d), F32),
        scratch_shapes=[pltpu.VMEM((tm, d), F32)],
        compiler_params=_cparams(("arbitrary", "arbitrary")),
        name="ffn_dense",
    )(h, wg, wu, wd, x2, gate, g)


def _ffn_expert_body(blk_e_ref, nvalid_ref, x_ref, wg_ref, wu_ref, wd_ref, o_ref, acc_ref):
    i = pl.program_id(0)
    j = pl.program_id(1)
    valid = i < nvalid_ref[0]

    @pl.when(j == 0)
    def _():
        acc_ref[...] = jnp.zeros(acc_ref.shape, F32)

    @pl.when(valid)
    def _():
        xb = x_ref[...]
        a = _nn(xb, wg_ref[0].astype(BF16))
        u = _nn(xb, wu_ref[0].astype(BF16))
        acc_ref[...] += _nn((a * _sigmoid(a) * u).astype(BF16), wd_ref[0].astype(BF16))

    @pl.when(j == pl.num_programs(1) - 1)
    def _():
        o_ref[...] = acc_ref[...]


def _ffn_expert(xb, wg, wu, wd, blk_e, nvalid, tmb):
    p, d = xb.shape
    nb = p // tmb
    tf = 512
    nf = D_FF // tf

    def jeff(i, j, nv):
        return jnp.where(i < nv[0], j, nf - 1)

    grid_spec = pltpu.PrefetchScalarGridSpec(
        num_scalar_prefetch=2,
        grid=(nb, nf),
        in_specs=[pl.BlockSpec((tmb, d), lambda i, j, be, nv: (i, 0)),
                  pl.BlockSpec((1, d, tf), lambda i, j, be, nv: (be[i], 0, jeff(i, j, nv))),
                  pl.BlockSpec((1, d, tf), lambda i, j, be, nv: (be[i], 0, jeff(i, j, nv))),
                  pl.BlockSpec((1, tf, d), lambda i, j, be, nv: (be[i], jeff(i, j, nv), 0))],
        out_specs=pl.BlockSpec((tmb, d), lambda i, j, be, nv: (i, 0)),
        scratch_shapes=[pltpu.VMEM((tmb, d), F32)],
    )
    return pl.pallas_call(
        _ffn_expert_body,
        grid_spec=grid_spec,
        out_shape=jax.ShapeDtypeStruct((p, d), F32),
        compiler_params=_cparams(("arbitrary", "arbitrary")),
        name="ffn_expert",
    )(blk_e, nvalid, xb, wg, wu, wd)


def _moe_combine_body(y1_ref, y2_ref, rg_ref, x_ref, gate_ref, g_ref, o_ref):
    rg = rg_ref[...]
    y = rg[:, 0:1] * y1_ref[...] + rg[:, 1:2] * y2_ref[...]
    o_ref[...] = x_ref[...] + gate_ref[0] * (_rms(y, D_MODEL) * g_ref[...])


def _moe_combine(y1, y2, rg, x2, gate, g, bsz, seq):
    n = bsz * seq
    d = D_MODEL
    tm = min(512, seq)
    tps = seq // tm
    row = lambda w: pl.BlockSpec((tm, w), lambda r: (r, 0))
    return pl.pallas_call(
        _moe_combine_body,
        grid=(n // tm,),
        in_specs=[row(d), row(d), row(LANES), row(d), pl.BlockSpec((1, 1, d), lambda r: (r // tps, 0, 0)),
                  pl.BlockSpec(g.shape, lambda r: (0, 0))],
        out_specs=row(d),
        out_shape=jax.ShapeDtypeStruct((n, d), F32),
        compiler_params=_cparams(("arbitrary",)),
        name="moe_combine",
    )(y1, y2, rg, x2, gate, g)


def _moe_ffn(h, ri, rg, counts, wg, wu, wd, x2, gate, g, bsz, seq):
    n = bsz * seq
    tmb = min(1024, n // 8)
    cnt = counts[0, :N_EXPERTS].astype(jnp.int32)
    padded = ((cnt + tmb - 1) // tmb) * tmb
    pend = jnp.cumsum(padded)
    pstart = pend - padded
    dest1 = pstart[ri[:, 0]] + ri[:, 2]
    dest2 = pstart[ri[:, 1]] + ri[:, 3]
    nb = (2 * n) // tmb + N_EXPERTS
    p = nb * tmb
    tok = jnp.arange(n, dtype=jnp.int32)
    buf_tok = jnp.zeros((p,), jnp.int32).at[dest1].set(tok).at[dest2].set(tok)
    nvalid = (pend[-1] // tmb).astype(jnp.int32).reshape(1)
    blk = jnp.minimum(jnp.arange(nb, dtype=jnp.int32), nvalid[0] - 1) * tmb
    blk_e = jnp.minimum(jnp.sum(blk[:, None] >= pend[None, :], axis=1), N_EXPERTS - 1).astype(jnp.int32)
    xb = jnp.take(h, buf_tok, axis=0)
    yb = _ffn_expert(xb, wg, wu, wd, blk_e, nvalid, tmb)
    y1 = jnp.take(yb, dest1, axis=0)
    y2 = jnp.take(yb, dest2, axis=0)
    return _moe_combine(y1, y2, rg, x2, gate, g, bsz, seq)


def _rot_cols(w):
    half = C_ROPE // 2
    return jnp.concatenate([-w[:, half:], w[:, :half]], axis=1)


def _prep_w_in(w_in):
    d = w_in.shape[0]
    z = lambda k: jnp.zeros((d, k), w_in.dtype)
    gates = w_in[:, 2304:2316]
    cq = w_in[:, 2316:2508]
    ckv = w_in[:, 2508:2636]
    kr = w_in[:, 2636:2668]
    cols = [w_in[:, :2304], cq, z(64), ckv,
            gates, z(C_NOPE - 12), kr, z(LANES - C_NOPE - C_ROPE),
            z(C_NOPE), _rot_cols(kr), z(LANES - C_NOPE - C_ROPE)]
    out = jnp.concatenate(cols, axis=1).astype(BF16)
    assert out.shape[1] == IN_COLS
    return out


def _prep_w_uq(w_uq):
    w = w_uq.reshape(C_Q_LORA, C_HEADS, C_NOPE + C_ROPE)
    nope, rp = w[..., :C_NOPE], w[..., C_NOPE:]
    pad = jnp.zeros((C_Q_LORA, C_HEADS, MLA_SLAB - C_NOPE - C_ROPE), w.dtype)
    rot = jnp.concatenate([-rp[..., C_ROPE // 2:], rp[..., :C_ROPE // 2]], axis=-1)
    wq = jnp.concatenate([nope, rp, pad], axis=-1).reshape(C_Q_LORA, MLA_QK_W)
    wqr = jnp.concatenate([jnp.zeros_like(nope), rot, pad], axis=-1).reshape(C_Q_LORA, MLA_QK_W)
    rowpad = jnp.zeros((256 - C_Q_LORA, MLA_QK_W), w.dtype)
    return (jnp.concatenate([wq, rowpad], axis=0).astype(BF16), jnp.concatenate([wqr, rowpad], axis=0).astype(BF16))


def _prep_w_ukv(w_ukv):
    w = w_ukv.reshape(C_KV_LORA, C_HEADS, C_NOPE + C_V_DIM)
    kn = jnp.concatenate([w[..., :C_NOPE], jnp.zeros((C_KV_LORA, C_HEADS, MLA_SLAB - C_NOPE), w.dtype)], axis=-1)
    v = w[..., C_NOPE:]
    return jnp.concatenate([kn.reshape(C_KV_LORA, MLA_QK_W), v.reshape(C_KV_LORA, MLA_V_W)], axis=1).astype(BF16)


def _rope_tables(positions):
    freqs = ROPE_THETA ** (-jnp.arange(0, C_ROPE, 2, dtype=F32) / C_ROPE)
    ang = positions.astype(F32).reshape(-1, 1) * freqs
    n = ang.shape[0]
    cos, sin = jnp.cos(ang), jnp.sin(ang)
    ones = lambda k: jnp.ones((n, k), F32)
    zeros = lambda k: jnp.zeros((n, k), F32)
    tail = MLA_SLAB - C_NOPE - C_ROPE
    return (jnp.concatenate([ones(C_NOPE), cos, cos, ones(tail)], axis=1),
            jnp.concatenate([zeros(C_NOPE), sin, sin, zeros(tail)], axis=1))


def kernel(x, c, positions, w_ada, b_ada, g_norm, w_in, w_out, lam_q1, lam_k1, lam_q2, lam_k2, g_diff, w_conv, b_conv, b_igate, b_fgate, g_mlstm, g_q_lat, g_kv_lat, w_uq, w_ukv, w_gate_d, w_up_d, w_down_d, w_router, b_router, w_gate_e, w_up_e, w_down_e):
    bsz, seq, d = x.shape
    n = bsz * seq
    nc = seq // CHUNK
    depth = w_in.shape[0]
    mods = _ada_all(c, w_ada, b_ada)
    cos_t, sin_t = _rope_tables(positions)
    x2 = x.reshape(n, d)
    row1 = lambda v: v.reshape(1, -1).astype(F32)

    for l in range(depth):
        mod = lambda k, j: mods[2 * l + k, j].reshape(bsz, 1, d)
        wq, wqr = _prep_w_uq(w_uq[l])
        gq = jnp.concatenate([g_q_lat[l], jnp.zeros((256 - C_Q_LORA,), F32)]).reshape(1, 256)
        (qd, kd, vd, qm, km, vm, om, gt, qc, kc, vc) = _mix_in(
            x2, mod(0, 0), mod(0, 1), row1(g_norm[l, 0]), _prep_w_in(w_in[l]), w_conv[l], row1(b_conv[l]),
            gq, row1(g_kv_lat[l]), wq, wqr, _prep_w_ukv(w_ukv[l]), cos_t, sin_t, bsz, seq)

        lam_init = 0.8 - 0.6 * math.exp(-0.3 * l)
        lamv = jnp.zeros((8, LANES), F32).at[0:4, 0:A_QK_DIM].set(jnp.stack([lam_q1[l], lam_k1[l], lam_q2[l], lam_k2[l]]))
        out_a = _diff_attn(qd, kd, vd, lamv, row1(jnp.tile(g_diff[l], A_HEADS)), lam_init, bsz, seq)

        z2 = jnp.zeros((bsz, 2, seq), F32)
        g16 = jnp.concatenate([gt[:, 0:6], z2, gt[:, 6:12], z2], axis=1)
        gates = jnp.pad(g16.reshape(bsz, 16, nc, CHUNK).transpose(0, 2, 1, 3), ((0, 0), (0, 0), (0, 0), (0, LANES - CHUNK)))
        zb = jnp.zeros((2,), F32)
        gbias = jnp.broadcast_to(jnp.concatenate([b_igate[l], zb, b_fgate[l], zb])[:, None], (16, LANES)).astype(F32)
        out_b = _mlstm(qm, km, vm, om, gates, gbias, row1(g_mlstm[l]), bsz, seq)

        out_c = _mla_attn(qc, kc, vc, bsz, seq)

        moe = (l % 2 == 1)
        router = None
        if moe:
            w_hi = w_router[l // 2].astype(BF16)
            w_lo = (w_router[l // 2] - w_hi.astype(F32)).astype(BF16)
            wr = jnp.pad(jnp.concatenate([w_hi, w_lo], axis=1), ((0, 0), (0, LANES - 2 * N_EXPERTS)))
            br = jnp.pad(b_router[l // 2], (0, LANES - N_EXPERTS)).reshape(1, LANES)
            router = (wr, br)
        res = _mix_out(out_a, out_b, out_c, w_out[l].astype(BF16), x2, mod(0, 2), row1(g_norm[l, 1]),
                       mod(1, 0), mod(1, 1), row1(g_norm[l, 2]), router, bsz, seq)
        if moe:
            x2, h2, ri, rg, counts = res
            x2 = _moe_ffn(h2, ri, rg, counts, w_gate_e[l // 2], w_up_e[l // 2], w_down_e[l // 2],
                          x2, mod(1, 2), row1(g_norm[l, 3]), bsz, seq)
        else:
            x2, h2 = res
            x2 = _ffn_dense(h2, w_gate_d[l // 2], w_up_d[l // 2], w_down_d[l // 2],
                            x2, mod(1, 2), row1(g_norm[l, 3]), bsz, seq)
    return x2.reshape(bsz, seq, d)
```

```python
import functools
import math

import jax
import jax.numpy as jnp
import numpy as np
from jax import lax
from jax.experimental import pallas as pl
from jax.experimental.pallas import tpu as pltpu

F32 = jnp.float32
BF16 = jnp.bfloat16
HIGHEST = lax.Precision.HIGHEST

D_MODEL = 1024
DEPTH = 2
CHUNK = 64
EPS = 1e-6
A_HEADS, A_QK_DIM, A_V_DIM = 4, 32, 64
B_HEADS, B_DIM, CONV_WIDTH = 6, 64, 4
C_HEADS, C_Q_LORA, C_KV_LORA, C_NOPE, C_ROPE, C_V_DIM = 6, 192, 128, 64, 32, 64
ROPE_THETA = 10000.0
D_FF = 3584
N_EXPERTS = 8
LOG2E = 1.4426950408889634
NEG_BIG = -1e30

LANES = 128
DIFF_W = 2 * A_HEADS * A_QK_DIM
MLSTM_W = B_HEADS * B_DIM
MLA_SLAB = 128
MLA_QK_W = C_HEADS * MLA_SLAB
MLA_V_W = C_HEADS * C_V_DIM

COL_DIFF = 0
COL_MLSTM = 768
COL_CQ = 2304
COL_CKV = 2560
COL_X = 2688
COL_Y = 2816
IN_COLS = 2944

VMEM_LIMIT = 56 * 1024 * 1024


def _cparams(sem):
    return pltpu.CompilerParams(dimension_semantics=sem, vmem_limit_bytes=VMEM_LIMIT)


def _nt(a, b):
    return lax.dot_general(a, b, (((1,), (1,)), ((), ())), preferred_element_type=F32)


def _nn(a, b):
    return jnp.dot(a, b, preferred_element_type=F32)


def _sigmoid(v):
    return 1.0 / (1.0 + jnp.exp(-v))


def _rms(v, n):
    return v * lax.rsqrt(jnp.sum(v * v, axis=-1, keepdims=True) * (1.0 / n) + EPS)


def _ada_body(c_ref, w_ref, b_ref, o_ref):
    c = c_ref[...]
    sc = c * _sigmoid(c)
    o_ref[0, 0] = jnp.dot(sc, w_ref[0], precision=HIGHEST, preferred_element_type=F32) + b_ref[0]


def _ada_all(c, w_ada, b_ada):
    bsz, d = c.shape
    n = w_ada.shape[0] * w_ada.shape[1]
    w = w_ada.reshape(n, d, 3 * d)
    b = b_ada.reshape(n, 1, 3 * d)
    return pl.pallas_call(
        _ada_body,
        grid=(n, 3),
        in_specs=[pl.BlockSpec((bsz, d), lambda i, j: (0, 0)),
                  pl.BlockSpec((1, d, d), lambda i, j: (i, 0, j)),
                  pl.BlockSpec((1, 1, d), lambda i, j: (i, 0, j))],
        out_specs=pl.BlockSpec((1, 1, bsz, d), lambda i, j: (i, j, 0, 0)),
        out_shape=jax.ShapeDtypeStruct((n, 3, bsz, d), F32),
        compiler_params=_cparams(("arbitrary", "arbitrary")),
        name="ada_mod",
    )(c, w, b)


def _mix_in_body(tiles_per_seq, tm,
                 x_ref, shift_ref, scale_ref, g_ref, win_ref, wconv_ref, bconv_ref, gq_ref, gkv_ref,
                 wq_ref, wqr_ref, wkv_ref, cos_ref, sin_ref,
                 qd_ref, kd_ref, vd_ref, qm_ref, km_ref, vm_ref, om_ref, gt_ref, qc_ref, kc_ref, vc_ref,
                 carry_ref):
    r = pl.program_id(0)
    x = x_ref[...]
    h = _rms(x, D_MODEL) * g_ref[...]
    h = h * (1.0 + scale_ref[0]) + shift_ref[0]
    p = _nn(h.astype(BF16), win_ref[...])

    qd_ref[...] = (p[:, 0:256] * (A_QK_DIM ** -0.5 * LOG2E)).astype(BF16)
    kd_ref[...] = p[:, 256:512].astype(BF16)
    vd_ref[...] = p[:, 512:768].astype(BF16)

    @pl.when(r % tiles_per_seq == 0)
    def _():
        carry_ref[0:8, :] = jnp.zeros((8, 2 * MLSTM_W), F32)

    carry_ref[8:8 + tm, :] = p[:, COL_MLSTM:COL_MLSTM + 2 * MLSTM_W]
    w = wconv_ref[...]
    conv = bconv_ref[...] + carry_ref[8:8 + tm, :] * w[3:4, :]
    for j in range(CONV_WIDTH - 1):
        conv = conv + carry_ref[5 + j:5 + j + tm, :] * w[j:j + 1, :]
    carry_ref[0:8, :] = carry_ref[tm:tm + 8, :]
    qk = conv * _sigmoid(conv)
    qm_ref[...] = qk[:, 0:MLSTM_W].astype(BF16)
    km_ref[...] = (qk[:, MLSTM_W:2 * MLSTM_W] * (B_DIM ** -0.5)).astype(BF16)
    vm_ref[...] = p[:, COL_MLSTM + 2 * MLSTM_W:COL_MLSTM + 3 * MLSTM_W].astype(BF16)
    om_ref[...] = p[:, COL_MLSTM + 3 * MLSTM_W:COL_MLSTM + 4 * MLSTM_W].astype(BF16)

    xg = p[:, COL_X:COL_X + LANES]
    gt_ref[0] = xg.T[0:16, :]

    cq = p[:, COL_CQ:COL_CQ + 256]
    cqn = (cq * lax.rsqrt(jnp.sum(cq * cq, axis=-1, keepdims=True) * (1.0 / C_Q_LORA) + EPS) * gq_ref[...]).astype(BF16)
    ckv = p[:, COL_CKV:COL_CKV + C_KV_LORA]
    ckvn = (_rms(ckv, C_KV_LORA) * gkv_ref[...]).astype(BF16)
    cos_t = cos_ref[...]
    sin_t = sin_ref[...]
    cos6 = jnp.concatenate([cos_t] * C_HEADS, axis=1)
    sin6 = jnp.concatenate([sin_t] * C_HEADS, axis=1)
    qn = _nn(cqn, wq_ref[...])
    qr = _nn(cqn, wqr_ref[...])
    qc_ref[...] = ((qn * cos6 + qr * sin6) * ((C_NOPE + C_ROPE) ** -0.5 * LOG2E)).astype(BF16)
    kvp = _nn(ckvn, wkv_ref[...])
    lane = lax.broadcasted_iota(jnp.int32, (1, LANES), 1)
    rope_lanes = (lane >= C_NOPE) & (lane < C_NOPE + C_ROPE)
    yg = p[:, COL_Y:COL_Y + LANES]
    krope = jnp.where(rope_lanes, xg * cos_t + yg * sin_t, 0.0)
    kc_ref[...] = (kvp[:, 0:MLA_QK_W] + jnp.concatenate([krope] * C_HEADS, axis=1)).astype(BF16)
    vc_ref[...] = kvp[:, MLA_QK_W:MLA_QK_W + MLA_V_W].astype(BF16)


def _mix_in(x2, shift, scale, g, win, wconv, bconv, gq, gkv, wq, wqr, wkv, cos_t, sin_t, bsz, seq):
    n = bsz * seq
    tm = min(512, seq)
    tps = seq // tm
    d = D_MODEL
    row = lambda w: pl.BlockSpec((tm, w), lambda r: (r, 0))
    full = lambda a: pl.BlockSpec(a.shape, lambda r: (0,) * a.ndim)
    mod = pl.BlockSpec((1, 1, d), lambda r: (r // tps, 0, 0))
    out_widths = [DIFF_W, DIFF_W, DIFF_W, MLSTM_W, MLSTM_W, MLSTM_W, MLSTM_W]
    out_shape = [jax.ShapeDtypeStruct((n, w), BF16) for w in out_widths]
    out_specs = [row(w) for w in out_widths]
    out_shape.append(jax.ShapeDtypeStruct((bsz, 16, seq), F32))
    out_specs.append(pl.BlockSpec((1, 16, tm), lambda r: (r // tps, 0, r % tps)))
    for w in (MLA_QK_W, MLA_QK_W, MLA_V_W):
        out_shape.append(jax.ShapeDtypeStruct((n, w), BF16))
        out_specs.append(row(w))
    return pl.pallas_call(
        functools.partial(_mix_in_body, tps, tm),
        grid=(n // tm,),
        in_specs=[row(d), mod, mod, full(g), full(win), full(wconv), full(bconv), full(gq), full(gkv),
                  full(wq), full(wqr), full(wkv), row(LANES), row(LANES)],
        out_specs=out_specs,
        out_shape=out_shape,
        scratch_shapes=[pltpu.VMEM((tm + 8, 2 * MLSTM_W), F32)],
        compiler_params=_cparams(("arbitrary",)),
        name="mix_in",
    )(x2, shift, scale, g, win, wconv, bconv, gq, gkv, wq, wqr, wkv, cos_t, sin_t)


def _chunk_mask(tq, tk):
    ri = lax.broadcasted_iota(jnp.int32, (tq, tk), 0)
    ci = lax.broadcasted_iota(jnp.int32, (tq, tk), 1)
    return (ci >> 6) <= (ri >> 6)


def _two_pass_softmax(i, nq, tk, score_fn, pv_fn, mask, s_ref, p_ref, mx_ref, l_ref):
    rows = mx_ref.shape[0]
    tq = mask.shape[0]
    mx_ref[...] = jnp.full(mx_ref.shape, NEG_BIG, F32)

    def pass1(j, masked):
        s = score_fn(j)
        if masked:
            s = jnp.where(mask, s.reshape(rows // tq, tq, tk), NEG_BIG).reshape(rows, tk)
        s_ref[j] = s
        mx_ref[...] = jnp.maximum(mx_ref[...], jnp.maximum(s[:, 0:LANES], s[:, LANES:2 * LANES]))

    def body1(j, c):
        pass1(j, False)
        return c

    lax.fori_loop(0, i, body1, 0)
    pass1(i, True)
    mx_ref[...] = jnp.broadcast_to(jnp.max(mx_ref[...], axis=1, keepdims=True), mx_ref.shape)

    rb = 128

    def pass2_pv(n):
        for j in range(n):
            for r0 in range(0, rows, rb):
                s = s_ref[j, r0:r0 + rb, :]
                m = mx_ref[r0:r0 + rb, :]
                p0 = jnp.exp2(s[:, 0:LANES] - m)
                p1 = jnp.exp2(s[:, LANES:2 * LANES] - m)
                l_ref[r0:r0 + rb, :] = (p0 + p1) if j == 0 else (l_ref[r0:r0 + rb, :] + (p0 + p1))
                p_ref[j, r0:r0 + rb, :] = jnp.concatenate([p0, p1], axis=1).astype(BF16)
        pv_fn(n)

    for n in range(1, nq + 1):
        @pl.when(i == n - 1)
        def _():
            pass2_pv(n)
    return jnp.sum(l_ref[...], axis=1, keepdims=True)


def _diff_attn_body(tq, nq, lam_init, q_ref, k_ref, v_ref, lam_ref, g_ref, o_ref,
                    qs_ref, s_ref, p_ref, mx_ref, l_ref, acc_ref):
    i = pl.program_id(1)
    tk = tq
    nslab = 2 * A_HEADS
    q = q_ref[...]
    lane = lax.broadcasted_iota(jnp.int32, (1, DIFF_W), 1)
    zero = jnp.zeros_like(q)
    for s in range(nslab):
        qs_ref[s * tq:(s + 1) * tq, :] = jnp.where((lane >= A_QK_DIM * s) & (lane < A_QK_DIM * (s + 1)), q, zero)
    mask = _chunk_mask(tq, tk)

    def score(j):
        return _nt(qs_ref[...], k_ref[pl.ds(pl.multiple_of(j * tk, tk), tk), :])

    def pv(n):
        acc_ref[...] = sum(_nn(p_ref[j], v_ref[j * tk:(j + 1) * tk, :]) for j in range(n))

    l = _two_pass_softmax(i, nq, tk, score, pv, mask, s_ref, p_ref, mx_ref, l_ref)

    lv = lam_ref[...]
    lam = (jnp.exp(jnp.sum(lv[0:1] * lv[1:2], axis=1, keepdims=True))
           - jnp.exp(jnp.sum(lv[2:3] * lv[3:4], axis=1, keepdims=True)) + lam_init)
    o = acc_ref[...] / l
    out = jnp.zeros((tq, DIFF_W), F32)
    for hd in range(A_HEADS):
        sel = (lane >= A_V_DIM * hd) & (lane < A_V_DIM * (hd + 1))
        oh = o[hd * tq:(hd + 1) * tq] - lam * o[(A_HEADS + hd) * tq:(A_HEADS + hd + 1) * tq]
        ms = jnp.sum(jnp.where(sel, oh * oh, 0.0), axis=1, keepdims=True) * (1.0 / A_V_DIM)
        out = out + jnp.where(sel, oh * lax.rsqrt(ms + EPS), 0.0)
    o_ref[...] = (out * g_ref[...] * (1.0 - lam_init)).astype(BF16)


def _diff_attn(qd, kd, vd, lamv, g4, lam_init, bsz, seq):
    tq = min(256, seq)
    nq = seq // tq
    nslab = 2 * A_HEADS
    rows = nslab * tq
    return pl.pallas_call(
        functools.partial(_diff_attn_body, tq, nq, lam_init),
        grid=(bsz, nq),
        in_specs=[pl.BlockSpec((tq, DIFF_W), lambda b, i: (b * nq + i, 0)),
                  pl.BlockSpec((seq, DIFF_W), lambda b, i: (b, 0)),
                  pl.BlockSpec((seq, DIFF_W), lambda b, i: (b, 0)),
                  pl.BlockSpec(lamv.shape, lambda b, i: (0, 0)),
                  pl.BlockSpec(g4.shape, lambda b, i: (0, 0))],
        out_specs=pl.BlockSpec((tq, DIFF_W), lambda b, i: (b * nq + i, 0)),
        out_shape=jax.ShapeDtypeStruct((bsz * seq, DIFF_W), BF16),
        scratch_shapes=[pltpu.VMEM((rows, DIFF_W), BF16), pltpu.VMEM((nq, rows, tq), F32),
                        pltpu.VMEM((nq, rows, tq), BF16), pltpu.VMEM((rows, LANES), F32),
                        pltpu.VMEM((rows, LANES), F32), pltpu.VMEM((rows, DIFF_W), F32)],
        compiler_params=_cparams(("arbitrary", "arbitrary")),
        name="diff_attn",
    )(qd, kd, vd, lamv, g4)


def _mla_attn_body(tq, nq, q_ref, k_ref, v_ref, o_ref, s_ref, p_ref, mx_ref, l_ref, acc1_ref, acc2_ref):
    i = pl.program_id(1)
    tk = tq
    mask = _chunk_mask(tq, tk)
    g1 = 4

    def score(j):
        kt = k_ref[pl.ds(pl.multiple_of(j * tk, tk), tk), :]
        return jnp.concatenate([_nt(q_ref[:, MLA_SLAB * hd:MLA_SLAB * (hd + 1)], kt[:, MLA_SLAB * hd:MLA_SLAB * (hd + 1)])
                                for hd in range(C_HEADS)], axis=0)

    def pv(n):
        acc1_ref[...] = sum(_nn(p_ref[j, 0:g1 * tq, :], v_ref[j * tk:(j + 1) * tk, 0:256]) for j in range(n))
        acc2_ref[...] = sum(_nn(p_ref[j, g1 * tq:, :], v_ref[j * tk:(j + 1) * tk, 256:MLA_V_W]) for j in range(n))

    l = _two_pass_softmax(i, nq, tk, score, pv, mask, s_ref, p_ref, mx_ref, l_ref)

    linv = 1.0 / l
    o1 = acc1_ref[...] * linv[0:g1 * tq]
    o2 = acc2_ref[...] * linv[g1 * tq:]
    lane1 = lax.broadcasted_iota(jnp.int32, (1, 256), 1)
    lane2 = lax.broadcasted_iota(jnp.int32, (1, 128), 1)
    out1 = jnp.zeros((tq, 256), F32)
    for hd in range(g1):
        out1 = out1 + jnp.where((lane1 >= C_V_DIM * hd) & (lane1 < C_V_DIM * (hd + 1)), o1[hd * tq:(hd + 1) * tq], 0.0)
    out2 = jnp.zeros((tq, 128), F32)
    for hd in range(C_HEADS - g1):
        out2 = out2 + jnp.where((lane2 >= C_V_DIM * hd) & (lane2 < C_V_DIM * (hd + 1)), o2[hd * tq:(hd + 1) * tq], 0.0)
    o_ref[...] = jnp.concatenate([out1, out2], axis=1).astype(BF16)


def _mla_attn(qc, kc, vc, bsz, seq):
    tq = min(256, seq)
    nq = seq // tq
    return pl.pallas_call(
        functools.partial(_mla_attn_body, tq, nq),
        grid=(bsz, nq),
        in_specs=[pl.BlockSpec((tq, MLA_QK_W), lambda b, i: (b * nq + i, 0)),
                  pl.BlockSpec((seq, MLA_QK_W), lambda b, i: (b, 0)),
                  pl.BlockSpec((seq, MLA_V_W), lambda b, i: (b, 0))],
        out_specs=pl.BlockSpec((tq, MLA_V_W), lambda b, i: (b * nq + i, 0)),
        out_shape=jax.ShapeDtypeStruct((bsz * seq, MLA_V_W), BF16),
        scratch_shapes=[pltpu.VMEM((nq, C_HEADS * tq, tq), F32), pltpu.VMEM((nq, C_HEADS * tq, tq), BF16),
                        pltpu.VMEM((C_HEADS * tq, LANES), F32), pltpu.VMEM((C_HEADS * tq, LANES), F32),
                        pltpu.VMEM((4 * tq, 256), F32), pltpu.VMEM((2 * tq, 128), F32)],
        compiler_params=_cparams(("arbitrary", "arbitrary")),
        name="mla_attn",
    )(qc, kc, vc)


def _mlstm_body(nc, q_ref, k_ref, v_ref, og_ref, gates_ref, gbias_ref, gout_ref, o_ref,
                ct_ref, m_ref, a_ref, b_ref, al_ref, bl_ref):
    ct_ref[...] = jnp.zeros(ct_ref.shape, F32)
    m_ref[...] = jnp.zeros(m_ref.shape, F32)
    lane = lax.broadcasted_iota(jnp.int32, (1, LANES), 1)
    ti = lax.broadcasted_iota(jnp.int32, (CHUNK, CHUNK), 0)
    si = lax.broadcasted_iota(jnp.int32, (CHUNK, CHUNK), 1)
    tril = si <= ti
    eye = si == ti
    eye128 = (lax.broadcasted_iota(jnp.int32, (LANES, LANES), 0)
              == lax.broadcasted_iota(jnp.int32, (LANES, LANES), 1)).astype(BF16)
    gbias = gbias_ref[...]
    gout = gout_ref[...]

    g = gates_ref[0]
    ig = (g[:, 0:8, :] + gbias[0:8]).reshape(nc * 8, LANES)
    fg = (g[:, 8:16, :] + gbias[8:16]).reshape(nc * 8, LANES)
    lf = jnp.minimum(fg, 0.0) - jnp.log(1.0 + jnp.exp(-jnp.abs(fg)))
    b_all = jnp.where(lane < CHUNK, lf, 0.0)
    for sh in (1, 2, 4, 8, 16, 32):
        b_all = b_all + jnp.where(lane >= sh, pltpu.roll(b_all, sh, 1), 0.0)
    a_all = ig - b_all
    a_ref[...] = a_all.reshape(nc, 8, LANES)
    b_ref[...] = b_all.reshape(nc, 8, LANES)
    al_ref[...] = jnp.broadcast_to(jnp.max(jnp.where(lane < CHUNK, a_all, NEG_BIG), axis=1, keepdims=True),
                                   (nc * 8, LANES)).reshape(nc, 8, LANES)
    bl_ref[...] = jnp.broadcast_to(jnp.sum(jnp.where(lane == CHUNK - 1, b_all, 0.0), axis=1, keepdims=True),
                                   (nc * 8, LANES)).reshape(nc, 8, LANES)

    def chunk(c, carry):
        rows = pl.ds(pl.multiple_of(c * CHUNK, CHUNK), CHUNK)
        a = a_ref[c]
        b = b_ref[c]
        b_last = bl_ref[c]

        for pr in range(B_HEADS // 2):
            cols = slice(LANES * pr, LANES * (pr + 1))
            qw = q_ref[rows, cols]
            kw = k_ref[rows, cols]
            vw = v_ref[rows, cols]
            kt = _nt(eye128, kw)
            ctp = ct_ref[pr]
            ctb = ctp.astype(BF16)
            outs = []
            for hh in range(2):
                hd = 2 * pr + hh
                half = (lane < CHUNK) if hh == 0 else (lane >= CHUNK)
                onecol = (lane == CHUNK) if hh == 0 else (lane == 0)
                qmk = jnp.where(half, qw, jnp.zeros_like(qw))
                vaug = jnp.where(half, vw, jnp.where(onecol, 1.0, 0.0).astype(BF16))
                a_h = a[hd:hd + 1, 0:CHUNK]
                b_h = b[hd:hd + 1, 0:CHUNK]
                full = (CHUNK, LANES)
                a_col = jnp.broadcast_to(jnp.max(jnp.where(tril, a_h, NEG_BIG), axis=1, keepdims=True), full)
                b_col = jnp.broadcast_to(jnp.sum(jnp.where(eye, b_h, 0.0), axis=1, keepdims=True), full)
                w0 = jnp.where(tril, jnp.exp(a_h - a_col[:, 0:CHUNK]), 0.0)
                sc0 = (_nt(qmk, kw) * w0).astype(BF16)
                x0 = _nn(sc0, vaug)
                xi = _nn(qmk, ctb)
                m = m_ref[hd:hd + 1, :]
                m_col = jnp.maximum(a_col, m)
                xs = jnp.exp(a_col - m_col) * x0 + jnp.exp(m - m_col) * xi
                den = jnp.broadcast_to(jnp.sum(jnp.where(onecol, xs, 0.0), axis=1, keepdims=True), full)
                hv = xs / jnp.maximum(jnp.abs(den), jnp.exp(-b_col - m_col))
                ms = jnp.broadcast_to(jnp.sum(jnp.where(half, hv * hv, 0.0), axis=1, keepdims=True), full)
                outs.append(hv * lax.rsqrt(ms * (1.0 / B_DIM) + EPS))
                al = al_ref[c][hd:hd + 1, :]
                m_l = jnp.maximum(al, m)
                kwt = (kt[CHUNK * hh:CHUNK * (hh + 1), :] * jnp.exp(a_h - al[:, 0:CHUNK])).astype(BF16)
                dct = _nn(kwt, vaug)
                ct_ref[pr, CHUNK * hh:CHUNK * (hh + 1), :] = (jnp.exp(m - m_l) * ctp[CHUNK * hh:CHUNK * (hh + 1)]
                                                              + jnp.exp(al - m_l) * dct)
                m_ref[hd:hd + 1, :] = b_last[hd:hd + 1, :] + m_l
            hn = jnp.where(lane < CHUNK, outs[0], outs[1])
            og = og_ref[rows, cols].astype(F32)
            o_ref[rows, cols] = (hn * gout[:, cols] * _sigmoid(og)).astype(BF16)
        return carry

    lax.fori_loop(0, nc, chunk, 0, unroll=2)


def _mlstm(qm, km, vm, om, gates, gbias, gout, bsz, seq):
    nc = seq // CHUNK
    blk = pl.BlockSpec((seq, MLSTM_W), lambda b: (b, 0))
    return pl.pallas_call(
        functools.partial(_mlstm_body, nc),
        grid=(bsz,),
        in_specs=[blk, blk, blk, blk,
                  pl.BlockSpec((1, nc, 16, LANES), lambda b: (b, 0, 0, 0)),
                  pl.BlockSpec(gbias.shape, lambda b: (0, 0)),
                  pl.BlockSpec(gout.shape, lambda b: (0, 0))],
        out_specs=blk,
        out_shape=jax.ShapeDtypeStruct((bsz * seq, MLSTM_W), BF16),
        scratch_shapes=[pltpu.VMEM((B_HEADS // 2, LANES, LANES), F32), pltpu.VMEM((8, LANES), F32)]
        + [pltpu.VMEM((nc, 8, LANES), F32)] * 4,
        compiler_params=_cparams(("arbitrary",)),
        name="mlstm",
    )(qm, km, vm, om, gates, gbias, gout)


def _mix_out_body(with_router, tm,
                  oa_ref, ob_ref, oc_ref, wout_ref, x_ref, gate_ref, g1_ref, shift_ref, scale_ref, g2_ref, *rest):
    if with_router:
        wr_ref, br_ref, xo_ref, h_ref, ri_ref, rg_ref, cnt_ref, run_ref = rest
    else:
        xo_ref, h_ref = rest
    na = A_HEADS * A_V_DIM
    y = (_nn(oa_ref[...], wout_ref[0:na, :]) + _nn(ob_ref[...], wout_ref[na:na + MLSTM_W, :])
         + _nn(oc_ref[...], wout_ref[na + MLSTM_W:, :]))
    xn = x_ref[...] + gate_ref[0] * (_rms(y, D_MODEL) * g1_ref[...])
    xo_ref[...] = xn
    h = _rms(xn, D_MODEL) * g2_ref[...]
    h = h * (1.0 + scale_ref[0]) + shift_ref[0]
    h_ref[...] = h.astype(h_ref.dtype)
    if with_router:
        @pl.when(pl.program_id(0) == 0)
        def _():
            run_ref[...] = jnp.zeros(run_ref.shape, F32)

        lane = lax.broadcasted_iota(jnp.int32, (1, LANES), 1)
        h_hi = h.astype(BF16)
        h_lo = (h - h_hi.astype(F32)).astype(BF16)
        part = _nn(h_hi, wr_ref[...]) + _nn(h_lo, wr_ref[...])
        logits = part + pltpu.roll(part, LANES - N_EXPERTS, 1) + br_ref[...]
        lg = jnp.where(lane < N_EXPERTS, logits, NEG_BIG)
        v1 = jnp.max(lg, axis=1, keepdims=True)
        i1 = jnp.min(jnp.where(lg == v1, lane, LANES), axis=1, keepdims=True)
        lg2 = jnp.where(lane == i1, NEG_BIG, lg)
        v2 = jnp.max(lg2, axis=1, keepdims=True)
        i2 = jnp.min(jnp.where(lg2 == v2, lane, LANES), axis=1, keepdims=True)
        gt1 = 1.0 / (1.0 + jnp.exp(v2 - v1))
        oh1 = lane == i1
        oh2 = lane == i2
        cnt = jnp.where(oh1 | oh2, 1.0, 0.0)
        ri_ = lax.broadcasted_iota(jnp.int32, (tm, tm), 0)
        ci_ = lax.broadcasted_iota(jnp.int32, (tm, tm), 1)
        lower = jnp.where(ci_ < ri_, 1.0, 0.0).astype(BF16)
        before = _nn(lower, cnt.astype(BF16)) + run_ref[0:1, :]
        r1 = jnp.sum(jnp.where(oh1, before, 0.0), axis=1, keepdims=True)
        r2 = jnp.sum(jnp.where(oh2, before, 0.0), axis=1, keepdims=True)
        run_ref[...] = run_ref[...] + jnp.sum(cnt, axis=0, keepdims=True)
        ri_ref[...] = jnp.where(lane == 0, i1, jnp.where(lane == 1, i2, jnp.where(
            lane == 2, r1.astype(jnp.int32), jnp.where(lane == 3, r2.astype(jnp.int32), 0))))
        rg_ref[...] = jnp.where(lane == 0, gt1, jnp.where(lane == 1, 1.0 - gt1, 0.0))
        cnt_ref[...] = run_ref[...]


def _mix_out(oa, ob, oc, wout, x2, gate, g1, shift, scale, g2, router, bsz, seq):
    n = bsz * seq
    d = D_MODEL
    tm = min(512, seq)
    tps = seq // tm
    row = lambda w: pl.BlockSpec((tm, w), lambda r: (r, 0))
    full = lambda a: pl.BlockSpec(a.shape, lambda r: (0,) * a.ndim)
    mod = pl.BlockSpec((1, 1, d), lambda r: (r // tps, 0, 0))
    in_specs = [row(oa.shape[1]), row(ob.shape[1]), row(oc.shape[1]), full(wout), row(d), mod, full(g1), mod, mod, full(g2)]
    args = [oa, ob, oc, wout, x2, gate, g1, shift, scale, g2]
    out_shape = [jax.ShapeDtypeStruct((n, d), F32), jax.ShapeDtypeStruct((n, d), BF16 if router is None else F32)]
    out_specs = [row(d), row(d)]
    scratch = []
    if router is not None:
        wr, br = router
        in_specs += [full(wr), full(br)]
        args += [wr, br]
        out_shape += [jax.ShapeDtypeStruct((n, LANES), jnp.int32), jax.ShapeDtypeStruct((n, LANES), F32),
                      jax.ShapeDtypeStruct((8, LANES), F32)]
        out_specs += [row(LANES), row(LANES), pl.BlockSpec((8, LANES), lambda r: (0, 0))]
        scratch = [pltpu.VMEM((8, LANES), F32)]
    return pl.pallas_call(
        functools.partial(_mix_out_body, router is not None, tm),
        grid=(n // tm,),
        in_specs=in_specs,
        out_specs=out_specs,
        out_shape=out_shape,
        scratch_shapes=scratch,
        compiler_params=_cparams(("arbitrary",)),
        name="mix_out_router" if router is not None else "mix_out",
    )(*args)


def _ffn_dense_body(h_ref, wg_ref, wu_ref, wd_ref, x_ref, gate_ref, g_ref, o_ref, acc_ref):
    j = pl.program_id(1)

    @pl.when(j == 0)
    def _():
        acc_ref[...] = jnp.zeros(acc_ref.shape, F32)

    h = h_ref[...]
    a = _nn(h, wg_ref[...].astype(BF16))
    u = _nn(h, wu_ref[...].astype(BF16))
    acc_ref[...] += _nn((a * _sigmoid(a) * u).astype(BF16), wd_ref[...].astype(BF16))

    @pl.when(j == pl.num_programs(1) - 1)
    def _():
        o_ref[...] = x_ref[...] + gate_ref[0] * (_rms(acc_ref[...], D_MODEL) * g_ref[...])


def _ffn_dense(h, wg, wu, wd, x2, gate, g, bsz, seq):
    n = bsz * seq
    d = D_MODEL
    tm = min(1024, seq)
    tps = seq // tm
    tf = 512
    return pl.pallas_call(
        _ffn_dense_body,
        grid=(n // tm, D_FF // tf),
        in_specs=[pl.BlockSpec((tm, d), lambda i, j: (i, 0)),
                  pl.BlockSpec((d, tf), lambda i, j: (0, j)),
                  pl.BlockSpec((d, tf), lambda i, j: (0, j)),
                  pl.BlockSpec((tf, d), lambda i, j: (j, 0)),
                  pl.BlockSpec((tm, d), lambda i, j: (i, 0)),
                  pl.BlockSpec((1, 1, d), lambda i, j: (i // tps, 0, 0)),
                  pl.BlockSpec(g.shape, lambda i, j: (0, 0))],
        out_specs=pl.BlockSpec((tm, d), lambda i, j: (i, 0)),
        out_shape=jax.ShapeDtypeStruct((n, d), F32),
        scratch_shapes=[pltpu.VMEM((tm, d), F32)],
        compiler_params=_cparams(("arbitrary", "arbitrary")),
        name="ffn_dense",
    )(h, wg, wu, wd, x2, gate, g)


def _ffn_expert_body(blk_e_ref, nvalid_ref, x_ref, wg_ref, wu_ref, wd_ref, o_ref, acc_ref):
    i = pl.program_id(0)
    j = pl.program_id(1)
    valid = i < nvalid_ref[0]

    @pl.when(j == 0)
    def _():
        acc_ref[...] = jnp.zeros(acc_ref.shape, F32)

    @pl.when(valid)
    def _():
        xb = x_ref[...].astype(BF16)
        a = _nn(xb, wg_ref[0].astype(BF16))
        u = _nn(xb, wu_ref[0].astype(BF16))
        acc_ref[...] += _nn((a * _sigmoid(a) * u).astype(BF16), wd_ref[0].astype(BF16))

    @pl.when(j == pl.num_programs(1) - 1)
    def _():
        o_ref[...] = acc_ref[...]


def _ffn_expert(xb, wg, wu, wd, blk_e, nvalid, tmb):
    p, d = xb.shape
    nb = p // tmb
    tf = 512
    nf = D_FF // tf

    def jeff(i, j, nv):
        return jnp.where(i < nv[0], j, nf - 1)

    grid_spec = pltpu.PrefetchScalarGridSpec(
        num_scalar_prefetch=2,
        grid=(nb, nf),
        in_specs=[pl.BlockSpec((tmb, d), lambda i, j, be, nv: (i, 0)),
                  pl.BlockSpec((1, d, tf), lambda i, j, be, nv: (be[i], 0, jeff(i, j, nv))),
                  pl.BlockSpec((1, d, tf), lambda i, j, be, nv: (be[i], 0, jeff(i, j, nv))),
                  pl.BlockSpec((1, tf, d), lambda i, j, be, nv: (be[i], jeff(i, j, nv), 0))],
        out_specs=pl.BlockSpec((tmb, d), lambda i, j, be, nv: (i, 0)),
        scratch_shapes=[pltpu.VMEM((tmb, d), F32)],
    )
    return pl.pallas_call(
        _ffn_expert_body,
        grid_spec=grid_spec,
        out_shape=jax.ShapeDtypeStruct((p, d), F32),
        compiler_params=_cparams(("arbitrary", "arbitrary")),
        name="ffn_expert",
    )(blk_e, nvalid, xb, wg, wu, wd)


DISPATCH_TOKENS = 512


def _moe_dispatch_body(dest_ref, h_ref, xb_in_ref, xb_ref, sem):
    del xb_in_ref
    base = pl.program_id(0) * DISPATCH_TOKENS

    def row_copy(t, k):
        return pltpu.make_async_copy(h_ref.at[pl.ds(base + t, 1), :], xb_ref.at[pl.ds(dest_ref[2 * t + k], 1), :], sem)

    def issue(t, c):
        row_copy(t, 0).start()
        row_copy(t, 1).start()
        return c

    def drain(t, c):
        row_copy(t, 0).wait()
        row_copy(t, 1).wait()
        return c

    lax.fori_loop(0, DISPATCH_TOKENS, issue, 0)
    lax.fori_loop(0, DISPATCH_TOKENS, drain, 0)


def _moe_dispatch(h, dest, p):
    n, d = h.shape
    td = DISPATCH_TOKENS
    return pl.pallas_call(
        _moe_dispatch_body,
        grid=(n // td,),
        in_specs=[pl.BlockSpec((2 * td,), lambda i: (i,), memory_space=pltpu.SMEM),
                  pl.BlockSpec(memory_space=pl.ANY),
                  pl.BlockSpec(memory_space=pl.ANY)],
        out_specs=pl.BlockSpec(memory_space=pl.ANY),
        out_shape=jax.ShapeDtypeStruct((p, d), h.dtype),
        scratch_shapes=[pltpu.SemaphoreType.DMA(())],
        input_output_aliases={2: 0},
        compiler_params=_cparams(("arbitrary",)),
        name="moe_dispatch",
    )(dest, h, jnp.zeros((p, d), h.dtype))


def _moe_combine_body(y1_ref, y2_ref, rg_ref, x_ref, gate_ref, g_ref, o_ref):
    rg = rg_ref[...]
    y = rg[:, 0:1] * y1_ref[...] + rg[:, 1:2] * y2_ref[...]
    o_ref[...] = x_ref[...] + gate_ref[0] * (_rms(y, D_MODEL) * g_ref[...])


def _moe_combine(y1, y2, rg, x2, gate, g, bsz, seq):
    n = bsz * seq
    d = D_MODEL
    tm = min(512, seq)
    tps = seq // tm
    row = lambda w: pl.BlockSpec((tm, w), lambda r: (r, 0))
    return pl.pallas_call(
        _moe_combine_body,
        grid=(n // tm,),
        in_specs=[row(d), row(d), row(LANES), row(d), pl.BlockSpec((1, 1, d), lambda r: (r // tps, 0, 0)),
                  pl.BlockSpec(g.shape, lambda r: (0, 0))],
        out_specs=row(d),
        out_shape=jax.ShapeDtypeStruct((n, d), F32),
        compiler_params=_cparams(("arbitrary",)),
        name="moe_combine",
    )(y1, y2, rg, x2, gate, g)


def _moe_ffn(h, ri, rg, counts, wg, wu, wd, x2, gate, g, bsz, seq):
    n = bsz * seq
    tmb = min(1024, n // 8)
    cnt = counts[0, :N_EXPERTS].astype(jnp.int32)
    padded = ((cnt + tmb - 1) // tmb) * tmb
    pend = jnp.cumsum(padded)
    pstart = pend - padded
    dest1 = pstart[ri[:, 0]] + ri[:, 2]
    dest2 = pstart[ri[:, 1]] + ri[:, 3]
    nb = (2 * n) // tmb + N_EXPERTS
    p = nb * tmb
    nvalid = (pend[-1] // tmb).astype(jnp.int32).reshape(1)
    blk = jnp.minimum(jnp.arange(nb, dtype=jnp.int32), nvalid[0] - 1) * tmb
    blk_e = jnp.minimum(jnp.sum(blk[:, None] >= pend[None, :], axis=1), N_EXPERTS - 1).astype(jnp.int32)
    xb = _moe_dispatch(h, jnp.stack([dest1, dest2], axis=1).reshape(2 * n), p)
    yb = _ffn_expert(xb, wg, wu, wd, blk_e, nvalid, tmb)
    y1 = jnp.take(yb, dest1, axis=0)
    y2 = jnp.take(yb, dest2, axis=0)
    return _moe_combine(y1, y2, rg, x2, gate, g, bsz, seq)


def _rot_cols(w):
    half = C_ROPE // 2
    return jnp.concatenate([-w[:, half:], w[:, :half]], axis=1)


def _prep_w_in(w_in):
    d = w_in.shape[0]
    z = lambda k: jnp.zeros((d, k), w_in.dtype)
    gates = w_in[:, 2304:2316]
    cq = w_in[:, 2316:2508]
    ckv = w_in[:, 2508:2636]
    kr = w_in[:, 2636:2668]
    cols = [w_in[:, :2304], cq, z(64), ckv,
            gates, z(C_NOPE - 12), kr, z(LANES - C_NOPE - C_ROPE),
            z(C_NOPE), _rot_cols(kr), z(LANES - C_NOPE - C_ROPE)]
    out = jnp.concatenate(cols, axis=1).astype(BF16)
    assert out.shape[1] == IN_COLS
    return out


def _prep_w_uq(w_uq):
    w = w_uq.reshape(C_Q_LORA, C_HEADS, C_NOPE + C_ROPE)
    nope, rp = w[..., :C_NOPE], w[..., C_NOPE:]
    pad = jnp.zeros((C_Q_LORA, C_HEADS, MLA_SLAB - C_NOPE - C_ROPE), w.dtype)
    rot = jnp.concatenate([-rp[..., C_ROPE // 2:], rp[..., :C_ROPE // 2]], axis=-1)
    wq = jnp.concatenate([nope, rp, pad], axis=-1).reshape(C_Q_LORA, MLA_QK_W)
    wqr = jnp.concatenate([jnp.zeros_like(nope), rot, pad], axis=-1).reshape(C_Q_LORA, MLA_QK_W)
    rowpad = jnp.zeros((256 - C_Q_LORA, MLA_QK_W), w.dtype)
    return (jnp.concatenate([wq, rowpad], axis=0).astype(BF16), jnp.concatenate([wqr, rowpad], axis=0).astype(BF16))


def _prep_w_ukv(w_ukv):
    w = w_ukv.reshape(C_KV_LORA, C_HEADS, C_NOPE + C_V_DIM)
    kn = jnp.concatenate([w[..., :C_NOPE], jnp.zeros((C_KV_LORA, C_HEADS, MLA_SLAB - C_NOPE), w.dtype)], axis=-1)
    v = w[..., C_NOPE:]
    return jnp.concatenate([kn.reshape(C_KV_LORA, MLA_QK_W), v.reshape(C_KV_LORA, MLA_V_W)], axis=1).astype(BF16)


def _rope_tables(positions):
    freqs = ROPE_THETA ** (-jnp.arange(0, C_ROPE, 2, dtype=F32) / C_ROPE)
    ang = positions.astype(F32).reshape(-1, 1) * freqs
    n = ang.shape[0]
    cos, sin = jnp.cos(ang), jnp.sin(ang)
    ones = lambda k: jnp.ones((n, k), F32)
    zeros = lambda k: jnp.zeros((n, k), F32)
    tail = MLA_SLAB - C_NOPE - C_ROPE
    return (jnp.concatenate([ones(C_NOPE), cos, cos, ones(tail)], axis=1),
            jnp.concatenate([zeros(C_NOPE), sin, sin, zeros(tail)], axis=1))


def kernel(x, c, positions, w_ada, b_ada, g_norm, w_in, w_out, lam_q1, lam_k1, lam_q2, lam_k2, g_diff, w_conv, b_conv, b_igate, b_fgate, g_mlstm, g_q_lat, g_kv_lat, w_uq, w_ukv, w_gate_d, w_up_d, w_down_d, w_router, b_router, w_gate_e, w_up_e, w_down_e):
    bsz, seq, d = x.shape
    n = bsz * seq
    nc = seq // CHUNK
    depth = w_in.shape[0]
    mods = _ada_all(c, w_ada, b_ada)
    cos_t, sin_t = _rope_tables(positions)
    x2 = x.reshape(n, d)
    row1 = lambda v: v.reshape(1, -1).astype(F32)

    for l in range(depth):
        mod = lambda k, j: mods[2 * l + k, j].reshape(bsz, 1, d)
        wq, wqr = _prep_w_uq(w_uq[l])
        gq = jnp.concatenate([g_q_lat[l], jnp.zeros((256 - C_Q_LORA,), F32)]).reshape(1, 256)
        (qd, kd, vd, qm, km, vm, om, gt, qc, kc, vc) = _mix_in(
            x2, mod(0, 0), mod(0, 1), row1(g_norm[l, 0]), _prep_w_in(w_in[l]), w_conv[l], row1(b_conv[l]),
            gq, row1(g_kv_lat[l]), wq, wqr, _prep_w_ukv(w_ukv[l]), cos_t, sin_t, bsz, seq)

        lam_init = 0.8 - 0.6 * math.exp(-0.3 * l)
        lamv = jnp.zeros((8, LANES), F32).at[0:4, 0:A_QK_DIM].set(jnp.stack([lam_q1[l], lam_k1[l], lam_q2[l], lam_k2[l]]))
        out_a = _diff_attn(qd, kd, vd, lamv, row1(jnp.tile(g_diff[l], A_HEADS)), lam_init, bsz, seq)

        z2 = jnp.zeros((bsz, 2, seq), F32)
        g16 = jnp.concatenate([gt[:, 0:6], z2, gt[:, 6:12], z2], axis=1)
        gates = jnp.pad(g16.reshape(bsz, 16, nc, CHUNK).transpose(0, 2, 1, 3), ((0, 0), (0, 0), (0, 0), (0, LANES - CHUNK)))
        zb = jnp.zeros((2,), F32)
        gbias = jnp.broadcast_to(jnp.concatenate([b_igate[l], zb, b_fgate[l], zb])[:, None], (16, LANES)).astype(F32)
        out_b = _mlstm(qm, km, vm, om, gates, gbias, row1(g_mlstm[l]), bsz, seq)

        out_c = _mla_attn(qc, kc, vc, bsz, seq)

        moe = (l % 2 == 1)
        router = None
        if moe:
            w_hi = w_router[l // 2].astype(BF16)
            w_lo = (w_router[l // 2] - w_hi.astype(F32)).astype(BF16)
            wr = jnp.pad(jnp.concatenate([w_hi, w_lo], axis=1), ((0, 0), (0, LANES - 2 * N_EXPERTS)))
            br = jnp.pad(b_router[l // 2], (0, LANES - N_EXPERTS)).reshape(1, LANES)
            router = (wr, br)
        res = _mix_out(out_a, out_b, out_c, w_out[l].astype(BF16), x2, mod(0, 2), row1(g_norm[l, 1]),
                       mod(1, 0), mod(1, 1), row1(g_norm[l, 2]), router, bsz, seq)
        if moe:
            x2, h2, ri, rg, counts = res
            x2 = _moe_ffn(h2, ri, rg, counts, w_gate_e[l // 2], w_up_e[l // 2], w_down_e[l // 2],
                          x2, mod(1, 2), row1(g_norm[l, 3]), bsz, seq)
        else:
            x2, h2 = res
            x2 = _ffn_dense(h2, w_gate_d[l // 2], w_up_d[l // 2], w_down_d[l // 2],
                            x2, mod(1, 2), row1(g_norm[l, 3]), bsz, seq)
    return x2.reshape(bsz, seq, d)
```

```python
import functools
import math

import jax
import jax.numpy as jnp
import numpy as np
from jax import lax
from jax.experimental import pallas as pl
from jax.experimental.pallas import tpu as pltpu

F32 = jnp.float32
BF16 = jnp.bfloat16
HIGHEST = lax.Precision.HIGHEST

D_MODEL = 1024
DEPTH = 2
CHUNK = 64
EPS = 1e-6
A_HEADS, A_QK_DIM, A_V_DIM = 4, 32, 64
B_HEADS, B_DIM, CONV_WIDTH = 6, 64, 4
C_HEADS, C_Q_LORA, C_KV_LORA, C_NOPE, C_ROPE, C_V_DIM = 6, 192, 128, 64, 32, 64
ROPE_THETA = 10000.0
D_FF = 3584
N_EXPERTS = 8
LOG2E = 1.4426950408889634
NEG_BIG = -1e30

LANES = 128
DIFF_W = 2 * A_HEADS * A_QK_DIM
MLSTM_W = B_HEADS * B_DIM
MLA_SLAB = 128
MLA_QK_W = C_HEADS * MLA_SLAB
MLA_V_W = C_HEADS * C_V_DIM

COL_DIFF = 0
COL_MLSTM = 768
COL_CQ = 2304
COL_CKV = 2560
COL_X = 2688
COL_Y = 2816
IN_COLS = 2944

VMEM_LIMIT = 56 * 1024 * 1024


def _cparams(sem):
    return pltpu.CompilerParams(dimension_semantics=sem, vmem_limit_bytes=VMEM_LIMIT)


def _nt(a, b):
    return lax.dot_general(a, b, (((1,), (1,)), ((), ())), preferred_element_type=F32)


def _nn(a, b):
    return jnp.dot(a, b, preferred_element_type=F32)


def _sigmoid(v):
    return 1.0 / (1.0 + jnp.exp(-v))


def _rms(v, n):
    return v * lax.rsqrt(jnp.sum(v * v, axis=-1, keepdims=True) * (1.0 / n) + EPS)


def _ada_body(c_ref, w_ref, b_ref, o_ref):
    c = c_ref[...]
    sc = c * _sigmoid(c)
    o_ref[0, 0] = jnp.dot(sc, w_ref[0], precision=HIGHEST, preferred_element_type=F32) + b_ref[0]


def _ada_all(c, w_ada, b_ada):
    bsz, d = c.shape
    n = w_ada.shape[0] * w_ada.shape[1]
    w = w_ada.reshape(n, d, 3 * d)
    b = b_ada.reshape(n, 1, 3 * d)
    return pl.pallas_call(
        _ada_body,
        grid=(n, 3),
        in_specs=[pl.BlockSpec((bsz, d), lambda i, j: (0, 0)),
                  pl.BlockSpec((1, d, d), lambda i, j: (i, 0, j)),
                  pl.BlockSpec((1, 1, d), lambda i, j: (i, 0, j))],
        out_specs=pl.BlockSpec((1, 1, bsz, d), lambda i, j: (i, j, 0, 0)),
        out_shape=jax.ShapeDtypeStruct((n, 3, bsz, d), F32),
        compiler_params=_cparams(("arbitrary", "arbitrary")),
        name="ada_mod",
    )(c, w, b)


def _mix_in_body(tiles_per_seq, tm,
                 x_ref, shift_ref, scale_ref, g_ref, win_ref, wconv_ref, bconv_ref, gq_ref, gkv_ref,
                 wq_ref, wqr_ref, wkv_ref, cos_ref, sin_ref,
                 qd_ref, kd_ref, vd_ref, qm_ref, km_ref, vm_ref, om_ref, gt_ref, qc_ref, kc_ref, vc_ref,
                 carry_ref):
    r = pl.program_id(0)
    x = x_ref[...]
    h = _rms(x, D_MODEL) * g_ref[...]
    h = h * (1.0 + scale_ref[0]) + shift_ref[0]
    p = _nn(h.astype(BF16), win_ref[...])

    qd_ref[...] = (p[:, 0:256] * (A_QK_DIM ** -0.5 * LOG2E)).astype(BF16)
    kd_ref[...] = p[:, 256:512].astype(BF16)
    vd_ref[...] = p[:, 512:768].astype(BF16)

    @pl.when(r % tiles_per_seq == 0)
    def _():
        carry_ref[0:8, :] = jnp.zeros((8, 2 * MLSTM_W), F32)

    carry_ref[8:8 + tm, :] = p[:, COL_MLSTM:COL_MLSTM + 2 * MLSTM_W]
    w = wconv_ref[...]
    conv = bconv_ref[...] + carry_ref[8:8 + tm, :] * w[3:4, :]
    for j in range(CONV_WIDTH - 1):
        conv = conv + carry_ref[5 + j:5 + j + tm, :] * w[j:j + 1, :]
    carry_ref[0:8, :] = carry_ref[tm:tm + 8, :]
    qk = conv * _sigmoid(conv)
    qm_ref[...] = qk[:, 0:MLSTM_W].astype(BF16)
    km_ref[...] = (qk[:, MLSTM_W:2 * MLSTM_W] * (B_DIM ** -0.5)).astype(BF16)
    vm_ref[...] = p[:, COL_MLSTM + 2 * MLSTM_W:COL_MLSTM + 3 * MLSTM_W].astype(BF16)
    om_ref[...] = p[:, COL_MLSTM + 3 * MLSTM_W:COL_MLSTM + 4 * MLSTM_W].astype(BF16)

    xg = p[:, COL_X:COL_X + LANES]
    gt_ref[0] = xg.T[0:16, :]

    cq = p[:, COL_CQ:COL_CQ + 256]
    cqn = (cq * lax.rsqrt(jnp.sum(cq * cq, axis=-1, keepdims=True) * (1.0 / C_Q_LORA) + EPS) * gq_ref[...]).astype(BF16)
    ckv = p[:, COL_CKV:COL_CKV + C_KV_LORA]
    ckvn = (_rms(ckv, C_KV_LORA) * gkv_ref[...]).astype(BF16)
    cos_t = cos_ref[...]
    sin_t = sin_ref[...]
    cos6 = jnp.concatenate([cos_t] * C_HEADS, axis=1)
    sin6 = jnp.concatenate([sin_t] * C_HEADS, axis=1)
    qn = _nn(cqn, wq_ref[...])
    qr = _nn(cqn, wqr_ref[...])
    qc_ref[...] = ((qn * cos6 + qr * sin6) * ((C_NOPE + C_ROPE) ** -0.5 * LOG2E)).astype(BF16)
    kvp = _nn(ckvn, wkv_ref[...])
    lane = lax.broadcasted_iota(jnp.int32, (1, LANES), 1)
    rope_lanes = (lane >= C_NOPE) & (lane < C_NOPE + C_ROPE)
    yg = p[:, COL_Y:COL_Y + LANES]
    krope = jnp.where(rope_lanes, xg * cos_t + yg * sin_t, 0.0)
    kc_ref[...] = (kvp[:, 0:MLA_QK_W] + jnp.concatenate([krope] * C_HEADS, axis=1)).astype(BF16)
    vc_ref[...] = kvp[:, MLA_QK_W:MLA_QK_W + MLA_V_W].astype(BF16)


def _mix_in(x2, shift, scale, g, win, wconv, bconv, gq, gkv, wq, wqr, wkv, cos_t, sin_t, bsz, seq):
    n = bsz * seq
    tm = min(512, seq)
    tps = seq // tm
    d = D_MODEL
    row = lambda w: pl.BlockSpec((tm, w), lambda r: (r, 0))
    full = lambda a: pl.BlockSpec(a.shape, lambda r: (0,) * a.ndim)
    mod = pl.BlockSpec((1, 1, d), lambda r: (r // tps, 0, 0))
    out_widths = [DIFF_W, DIFF_W, DIFF_W, MLSTM_W, MLSTM_W, MLSTM_W, MLSTM_W]
    out_shape = [jax.ShapeDtypeStruct((n, w), BF16) for w in out_widths]
    out_specs = [row(w) for w in out_widths]
    out_shape.append(jax.ShapeDtypeStruct((bsz, 16, seq), F32))
    out_specs.append(pl.BlockSpec((1, 16, tm), lambda r: (r // tps, 0, r % tps)))
    for w in (MLA_QK_W, MLA_QK_W, MLA_V_W):
        out_shape.append(jax.ShapeDtypeStruct((n, w), BF16))
        out_specs.append(row(w))
    return pl.pallas_call(
        functools.partial(_mix_in_body, tps, tm),
        grid=(n // tm,),
        in_specs=[row(d), mod, mod, full(g), full(win), full(wconv), full(bconv), full(gq), full(gkv),
                  full(wq), full(wqr), full(wkv), row(LANES), row(LANES)],
        out_specs=out_specs,
        out_shape=out_shape,
        scratch_shapes=[pltpu.VMEM((tm + 8, 2 * MLSTM_W), F32)],
        compiler_params=_cparams(("arbitrary",)),
        name="mix_in",
    )(x2, shift, scale, g, win, wconv, bconv, gq, gkv, wq, wqr, wkv, cos_t, sin_t)


def _chunk_mask(tq, tk):
    ri = lax.broadcasted_iota(jnp.int32, (tq, tk), 0)
    ci = lax.broadcasted_iota(jnp.int32, (tq, tk), 1)
    return (ci >> 6) <= (ri >> 6)


def _two_pass_softmax(i, nq, tk, score_fn, pv_fn, mask, s_ref, p_ref, mx_ref, l_ref):
    rows = mx_ref.shape[0]
    tq = mask.shape[0]
    mx_ref[...] = jnp.full(mx_ref.shape, NEG_BIG, F32)

    def pass1(j, masked):
        s = score_fn(j)
        if masked:
            s = jnp.where(mask, s.reshape(rows // tq, tq, tk), NEG_BIG).reshape(rows, tk)
        s_ref[j] = s
        mx_ref[...] = jnp.maximum(mx_ref[...], jnp.maximum(s[:, 0:LANES], s[:, LANES:2 * LANES]))

    def body1(j, c):
        pass1(j, False)
        return c

    lax.fori_loop(0, i, body1, 0)
    pass1(i, True)
    mx_ref[...] = jnp.broadcast_to(jnp.max(mx_ref[...], axis=1, keepdims=True), mx_ref.shape)

    rb = 128

    def pass2_pv(n):
        for j in range(n):
            for r0 in range(0, rows, rb):
                s = s_ref[j, r0:r0 + rb, :]
                m = mx_ref[r0:r0 + rb, :]
                p0 = jnp.exp2(s[:, 0:LANES] - m)
                p1 = jnp.exp2(s[:, LANES:2 * LANES] - m)
                l_ref[r0:r0 + rb, :] = (p0 + p1) if j == 0 else (l_ref[r0:r0 + rb, :] + (p0 + p1))
                p_ref[j, r0:r0 + rb, :] = jnp.concatenate([p0, p1], axis=1).astype(BF16)
        pv_fn(n)

    for n in range(1, nq + 1):
        @pl.when(i == n - 1)
        def _():
            pass2_pv(n)
    return jnp.sum(l_ref[...], axis=1, keepdims=True)


def _diff_attn_body(tq, nq, lam_init, q_ref, k_ref, v_ref, lam_ref, g_ref, o_ref,
                    qs_ref, s_ref, p_ref, mx_ref, l_ref, acc_ref):
    i = pl.program_id(1)
    tk = tq
    nslab = 2 * A_HEADS
    q = q_ref[...]
    lane = lax.broadcasted_iota(jnp.int32, (1, DIFF_W), 1)
    zero = jnp.zeros_like(q)
    for s in range(nslab):
        qs_ref[s * tq:(s + 1) * tq, :] = jnp.where((lane >= A_QK_DIM * s) & (lane < A_QK_DIM * (s + 1)), q, zero)
    mask = _chunk_mask(tq, tk)

    def score(j):
        return _nt(qs_ref[...], k_ref[pl.ds(pl.multiple_of(j * tk, tk), tk), :])

    def pv(n):
        acc_ref[...] = sum(_nn(p_ref[j], v_ref[j * tk:(j + 1) * tk, :]) for j in range(n))

    l = _two_pass_softmax(i, nq, tk, score, pv, mask, s_ref, p_ref, mx_ref, l_ref)

    lv = lam_ref[...]
    lam = (jnp.exp(jnp.sum(lv[0:1] * lv[1:2], axis=1, keepdims=True))
           - jnp.exp(jnp.sum(lv[2:3] * lv[3:4], axis=1, keepdims=True)) + lam_init)
    o = acc_ref[...] / l
    out = jnp.zeros((tq, DIFF_W), F32)
    for hd in range(A_HEADS):
        sel = (lane >= A_V_DIM * hd) & (lane < A_V_DIM * (hd + 1))
        oh = o[hd * tq:(hd + 1) * tq] - lam * o[(A_HEADS + hd) * tq:(A_HEADS + hd + 1) * tq]
        ms = jnp.sum(jnp.where(sel, oh * oh, 0.0), axis=1, keepdims=True) * (1.0 / A_V_DIM)
        out = out + jnp.where(sel, oh * lax.rsqrt(ms + EPS), 0.0)
    o_ref[...] = (out * g_ref[...] * (1.0 - lam_init)).astype(BF16)


def _diff_attn(qd, kd, vd, lamv, g4, lam_init, bsz, seq):
    tq = min(256, seq)
    nq = seq // tq
    nslab = 2 * A_HEADS
    rows = nslab * tq
    return pl.pallas_call(
        functools.partial(_diff_attn_body, tq, nq, lam_init),
        grid=(bsz, nq),
        in_specs=[pl.BlockSpec((tq, DIFF_W), lambda b, i: (b * nq + i, 0)),
                  pl.BlockSpec((seq, DIFF_W), lambda b, i: (b, 0)),
                  pl.BlockSpec((seq, DIFF_W), lambda b, i: (b, 0)),
                  pl.BlockSpec(lamv.shape, lambda b, i: (0, 0)),
                  pl.BlockSpec(g4.shape, lambda b, i: (0, 0))],
        out_specs=pl.BlockSpec((tq, DIFF_W), lambda b, i: (b * nq + i, 0)),
        out_shape=jax.ShapeDtypeStruct((bsz * seq, DIFF_W), BF16),
        scratch_shapes=[pltpu.VMEM((rows, DIFF_W), BF16), pltpu.VMEM((nq, rows, tq), F32),
                        pltpu.VMEM((nq, rows, tq), BF16), pltpu.VMEM((rows, LANES), F32),
                        pltpu.VMEM((rows, LANES), F32), pltpu.VMEM((rows, DIFF_W), F32)],
        compiler_params=_cparams(("arbitrary", "arbitrary")),
        name="diff_attn",
    )(qd, kd, vd, lamv, g4)


def _mla_attn_body(tq, nq, q_ref, k_ref, v_ref, o_ref, s_ref, p_ref, mx_ref, l_ref, acc1_ref, acc2_ref):
    i = pl.program_id(1)
    tk = tq
    mask = _chunk_mask(tq, tk)
    g1 = 4

    def score(j):
        kt = k_ref[pl.ds(pl.multiple_of(j * tk, tk), tk), :]
        return jnp.concatenate([_nt(q_ref[:, MLA_SLAB * hd:MLA_SLAB * (hd + 1)], kt[:, MLA_SLAB * hd:MLA_SLAB * (hd + 1)])
                                for hd in range(C_HEADS)], axis=0)

    def pv(n):
        acc1_ref[...] = sum(_nn(p_ref[j, 0:g1 * tq, :], v_ref[j * tk:(j + 1) * tk, 0:256]) for j in range(n))
        acc2_ref[...] = sum(_nn(p_ref[j, g1 * tq:, :], v_ref[j * tk:(j + 1) * tk, 256:MLA_V_W]) for j in range(n))

    l = _two_pass_softmax(i, nq, tk, score, pv, mask, s_ref, p_ref, mx_ref, l_ref)

    linv = 1.0 / l
    o1 = acc1_ref[...] * linv[0:g1 * tq]
    o2 = acc2_ref[...] * linv[g1 * tq:]
    lane1 = lax.broadcasted_iota(jnp.int32, (1, 256), 1)
    lane2 = lax.broadcasted_iota(jnp.int32, (1, 128), 1)
    out1 = jnp.zeros((tq, 256), F32)
    for hd in range(g1):
        out1 = out1 + jnp.where((lane1 >= C_V_DIM * hd) & (lane1 < C_V_DIM * (hd + 1)), o1[hd * tq:(hd + 1) * tq], 0.0)
    out2 = jnp.zeros((tq, 128), F32)
    for hd in range(C_HEADS - g1):
        out2 = out2 + jnp.where((lane2 >= C_V_DIM * hd) & (lane2 < C_V_DIM * (hd + 1)), o2[hd * tq:(hd + 1) * tq], 0.0)
    o_ref[...] = jnp.concatenate([out1, out2], axis=1).astype(BF16)


def _mla_attn(qc, kc, vc, bsz, seq):
    tq = min(256, seq)
    nq = seq // tq
    return pl.pallas_call(
        functools.partial(_mla_attn_body, tq, nq),
        grid=(bsz, nq),
        in_specs=[pl.BlockSpec((tq, MLA_QK_W), lambda b, i: (b * nq + i, 0)),
                  pl.BlockSpec((seq, MLA_QK_W), lambda b, i: (b, 0)),
                  pl.BlockSpec((seq, MLA_V_W), lambda b, i: (b, 0))],
        out_specs=pl.BlockSpec((tq, MLA_V_W), lambda b, i: (b * nq + i, 0)),
        out_shape=jax.ShapeDtypeStruct((bsz * seq, MLA_V_W), BF16),
        scratch_shapes=[pltpu.VMEM((nq, C_HEADS * tq, tq), F32), pltpu.VMEM((nq, C_HEADS * tq, tq), BF16),
                        pltpu.VMEM((C_HEADS * tq, LANES), F32), pltpu.VMEM((C_HEADS * tq, LANES), F32),
                        pltpu.VMEM((4 * tq, 256), F32), pltpu.VMEM((2 * tq, 128), F32)],
        compiler_params=_cparams(("arbitrary", "arbitrary")),
        name="mla_attn",
    )(qc, kc, vc)


def _mlstm_body(nc, q_ref, k_ref, v_ref, og_ref, gates_ref, gbias_ref, gout_ref, o_ref,
                ct_ref, m_ref, a_ref, b_ref, al_ref, bl_ref):
    ct_ref[...] = jnp.zeros(ct_ref.shape, F32)
    m_ref[...] = jnp.zeros(m_ref.shape, F32)
    lane = lax.broadcasted_iota(jnp.int32, (1, LANES), 1)
    ti = lax.broadcasted_iota(jnp.int32, (CHUNK, CHUNK), 0)
    si = lax.broadcasted_iota(jnp.int32, (CHUNK, CHUNK), 1)
    tril = si <= ti
    eye = si == ti
    eye128 = (lax.broadcasted_iota(jnp.int32, (LANES, LANES), 0)
              == lax.broadcasted_iota(jnp.int32, (LANES, LANES), 1)).astype(BF16)
    gbias = gbias_ref[...]
    gout = gout_ref[...]

    g = gates_ref[0]
    ig = (g[:, 0:8, :] + gbias[0:8]).reshape(nc * 8, LANES)
    fg = (g[:, 8:16, :] + gbias[8:16]).reshape(nc * 8, LANES)
    lf = jnp.minimum(fg, 0.0) - jnp.log(1.0 + jnp.exp(-jnp.abs(fg)))
    b_all = jnp.where(lane < CHUNK, lf, 0.0)
    for sh in (1, 2, 4, 8, 16, 32):
        b_all = b_all + jnp.where(lane >= sh, pltpu.roll(b_all, sh, 1), 0.0)
    a_all = ig - b_all
    a_ref[...] = a_all.reshape(nc, 8, LANES)
    b_ref[...] = b_all.reshape(nc, 8, LANES)
    al_ref[...] = jnp.broadcast_to(jnp.max(jnp.where(lane < CHUNK, a_all, NEG_BIG), axis=1, keepdims=True),
                                   (nc * 8, LANES)).reshape(nc, 8, LANES)
    bl_ref[...] = jnp.broadcast_to(jnp.sum(jnp.where(lane == CHUNK - 1, b_all, 0.0), axis=1, keepdims=True),
                                   (nc * 8, LANES)).reshape(nc, 8, LANES)

    def chunk(c, carry):
        rows = pl.ds(pl.multiple_of(c * CHUNK, CHUNK), CHUNK)
        a = a_ref[c]
        b = b_ref[c]
        b_last = bl_ref[c]

        for pr in range(B_HEADS // 2):
            cols = slice(LANES * pr, LANES * (pr + 1))
            qw = q_ref[rows, cols]
            kw = k_ref[rows, cols]
            vw = v_ref[rows, cols]
            kt = _nt(eye128, kw)
            ctp = ct_ref[pr]
            ctb = ctp.astype(BF16)
            outs = []
            for hh in range(2):
                hd = 2 * pr + hh
                half = (lane < CHUNK) if hh == 0 else (lane >= CHUNK)
                onecol = (lane == CHUNK) if hh == 0 else (lane == 0)
                qmk = jnp.where(half, qw, jnp.zeros_like(qw))
                vaug = jnp.where(half, vw, jnp.where(onecol, 1.0, 0.0).astype(BF16))
                a_h = a[hd:hd + 1, 0:CHUNK]
                b_h = b[hd:hd + 1, 0:CHUNK]
                full = (CHUNK, LANES)
                a_col = jnp.broadcast_to(jnp.max(jnp.where(tril, a_h, NEG_BIG), axis=1, keepdims=True), full)
                b_col = jnp.broadcast_to(jnp.sum(jnp.where(eye, b_h, 0.0), axis=1, keepdims=True), full)
                w0 = jnp.where(tril, jnp.exp(a_h - a_col[:, 0:CHUNK]), 0.0)
                sc0 = (_nt(qmk, kw) * w0).astype(BF16)
                x0 = _nn(sc0, vaug)
                xi = _nn(qmk, ctb)
                m = m_ref[hd:hd + 1, :]
                m_col = jnp.maximum(a_col, m)
                xs = jnp.exp(a_col - m_col) * x0 + jnp.exp(m - m_col) * xi
                den = jnp.broadcast_to(jnp.sum(jnp.where(onecol, xs, 0.0), axis=1, keepdims=True), full)
                hv = xs / jnp.maximum(jnp.abs(den), jnp.exp(-b_col - m_col))
                ms = jnp.broadcast_to(jnp.sum(jnp.where(half, hv * hv, 0.0), axis=1, keepdims=True), full)
                outs.append(hv * lax.rsqrt(ms * (1.0 / B_DIM) + EPS))
                al = al_ref[c][hd:hd + 1, :]
                m_l = jnp.maximum(al, m)
                kwt = (kt[CHUNK * hh:CHUNK * (hh + 1), :] * jnp.exp(a_h - al[:, 0:CHUNK])).astype(BF16)
                dct = _nn(kwt, vaug)
                ct_ref[pr, CHUNK * hh:CHUNK * (hh + 1), :] = (jnp.exp(m - m_l) * ctp[CHUNK * hh:CHUNK * (hh + 1)]
                                                              + jnp.exp(al - m_l) * dct)
                m_ref[hd:hd + 1, :] = b_last[hd:hd + 1, :] + m_l
            hn = jnp.where(lane < CHUNK, outs[0], outs[1])
            og = og_ref[rows, cols].astype(F32)
            o_ref[rows, cols] = (hn * gout[:, cols] * _sigmoid(og)).astype(BF16)
        return carry

    lax.fori_loop(0, nc, chunk, 0, unroll=2)


def _mlstm(qm, km, vm, om, gates, gbias, gout, bsz, seq):
    nc = seq // CHUNK
    blk = pl.BlockSpec((seq, MLSTM_W), lambda b: (b, 0))
    return pl.pallas_call(
        functools.partial(_mlstm_body, nc),
        grid=(bsz,),
        in_specs=[blk, blk, blk, blk,
                  pl.BlockSpec((1, nc, 16, LANES), lambda b: (b, 0, 0, 0)),
                  pl.BlockSpec(gbias.shape, lambda b: (0, 0)),
                  pl.BlockSpec(gout.shape, lambda b: (0, 0))],
        out_specs=blk,
        out_shape=jax.ShapeDtypeStruct((bsz * seq, MLSTM_W), BF16),
        scratch_shapes=[pltpu.VMEM((B_HEADS // 2, LANES, LANES), F32), pltpu.VMEM((8, LANES), F32)]
        + [pltpu.VMEM((nc, 8, LANES), F32)] * 4,
        compiler_params=_cparams(("arbitrary",)),
        name="mlstm",
    )(qm, km, vm, om, gates, gbias, gout)


def _mix_out_body(with_router, tm,
                  oa_ref, ob_ref, oc_ref, wout_ref, x_ref, gate_ref, g1_ref, shift_ref, scale_ref, g2_ref, *rest):
    if with_router:
        wr_ref, br_ref, xo_ref, h_ref, ri_ref, rg_ref, cnt_ref, run_ref = rest
    else:
        xo_ref, h_ref = rest
    na = A_HEADS * A_V_DIM
    y = (_nn(oa_ref[...], wout_ref[0:na, :]) + _nn(ob_ref[...], wout_ref[na:na + MLSTM_W, :])
         + _nn(oc_ref[...], wout_ref[na + MLSTM_W:, :]))
    xn = x_ref[...] + gate_ref[0] * (_rms(y, D_MODEL) * g1_ref[...])
    xo_ref[...] = xn
    h = _rms(xn, D_MODEL) * g2_ref[...]
    h = h * (1.0 + scale_ref[0]) + shift_ref[0]
    h_ref[...] = h.astype(h_ref.dtype)
    if with_router:
        @pl.when(pl.program_id(0) == 0)
        def _():
            run_ref[...] = jnp.zeros(run_ref.shape, F32)

        lane = lax.broadcasted_iota(jnp.int32, (1, LANES), 1)
        h_hi = h.astype(BF16)
        h_lo = (h - h_hi.astype(F32)).astype(BF16)
        part = _nn(h_hi, wr_ref[...]) + _nn(h_lo, wr_ref[...])
        logits = part + pltpu.roll(part, LANES - N_EXPERTS, 1) + br_ref[...]
        lg = jnp.where(lane < N_EXPERTS, logits, NEG_BIG)
        v1 = jnp.max(lg, axis=1, keepdims=True)
        i1 = jnp.min(jnp.where(lg == v1, lane, LANES), axis=1, keepdims=True)
        lg2 = jnp.where(lane == i1, NEG_BIG, lg)
        v2 = jnp.max(lg2, axis=1, keepdims=True)
        i2 = jnp.min(jnp.where(lg2 == v2, lane, LANES), axis=1, keepdims=True)
        gt1 = 1.0 / (1.0 + jnp.exp(v2 - v1))
        oh1 = lane == i1
        oh2 = lane == i2
        cnt = jnp.where(oh1 | oh2, 1.0, 0.0)
        ri_ = lax.broadcasted_iota(jnp.int32, (tm, tm), 0)
        ci_ = lax.broadcasted_iota(jnp.int32, (tm, tm), 1)
        lower = jnp.where(ci_ < ri_, 1.0, 0.0).astype(BF16)
        before = _nn(lower, cnt.astype(BF16)) + run_ref[0:1, :]
        r1 = jnp.sum(jnp.where(oh1, before, 0.0), axis=1, keepdims=True)
        r2 = jnp.sum(jnp.where(oh2, before, 0.0), axis=1, keepdims=True)
        run_ref[...] = run_ref[...] + jnp.sum(cnt, axis=0, keepdims=True)
        ri_ref[...] = jnp.where(lane == 0, i1, jnp.where(lane == 1, i2, jnp.where(
            lane == 2, r1.astype(jnp.int32), jnp.where(lane == 3, r2.astype(jnp.int32), 0))))
        rg_ref[...] = jnp.where(lane == 0, gt1, jnp.where(lane == 1, 1.0 - gt1, 0.0))
        cnt_ref[...] = run_ref[...]


def _mix_out(oa, ob, oc, wout, x2, gate, g1, shift, scale, g2, router, bsz, seq):
    n = bsz * seq
    d = D_MODEL
    tm = min(512, seq)
    tps = seq // tm
    row = lambda w: pl.BlockSpec((tm, w), lambda r: (r, 0))
    full = lambda a: pl.BlockSpec(a.shape, lambda r: (0,) * a.ndim)
    mod = pl.BlockSpec((1, 1, d), lambda r: (r // tps, 0, 0))
    in_specs = [row(oa.shape[1]), row(ob.shape[1]), row(oc.shape[1]), full(wout), row(d), mod, full(g1), mod, mod, full(g2)]
    args = [oa, ob, oc, wout, x2, gate, g1, shift, scale, g2]
    out_shape = [jax.ShapeDtypeStruct((n, d), F32), jax.ShapeDtypeStruct((n, d), BF16 if router is None else F32)]
    out_specs = [row(d), row(d)]
    scratch = []
    if router is not None:
        wr, br = router
        in_specs += [full(wr), full(br)]
        args += [wr, br]
        out_shape += [jax.ShapeDtypeStruct((n, LANES), jnp.int32), jax.ShapeDtypeStruct((n, LANES), F32),
                      jax.ShapeDtypeStruct((8, LANES), F32)]
        out_specs += [row(LANES), row(LANES), pl.BlockSpec((8, LANES), lambda r: (0, 0))]
        scratch = [pltpu.VMEM((8, LANES), F32)]
    return pl.pallas_call(
        functools.partial(_mix_out_body, router is not None, tm),
        grid=(n // tm,),
        in_specs=in_specs,
        out_specs=out_specs,
        out_shape=out_shape,
        scratch_shapes=scratch,
        compiler_params=_cparams(("arbitrary",)),
        name="mix_out_router" if router is not None else "mix_out",
    )(*args)


def _ffn_dense_body(h_ref, wg_ref, wu_ref, wd_ref, x_ref, gate_ref, g_ref, o_ref, acc_ref):
    j = pl.program_id(1)

    @pl.when(j == 0)
    def _():
        acc_ref[...] = jnp.zeros(acc_ref.shape, F32)

    h = h_ref[...]
    a = _nn(h, wg_ref[...].astype(BF16))
    u = _nn(h, wu_ref[...].astype(BF16))
    acc_ref[...] += _nn((a * _sigmoid(a) * u).astype(BF16), wd_ref[...].astype(BF16))

    @pl.when(j == pl.num_programs(1) - 1)
    def _():
        o_ref[...] = x_ref[...] + gate_ref[0] * (_rms(acc_ref[...], D_MODEL) * g_ref[...])


def _ffn_dense(h, wg, wu, wd, x2, gate, g, bsz, seq):
    n = bsz * seq
    d = D_MODEL
    tm = min(1024, seq)
    tps = seq // tm
    tf = 512
    return pl.pallas_call(
        _ffn_dense_body,
        grid=(n // tm, D_FF // tf),
        in_specs=[pl.BlockSpec((tm, d), lambda i, j: (i, 0)),
                  pl.BlockSpec((d, tf), lambda i, j: (0, j)),
                  pl.BlockSpec((d, tf), lambda i, j: (0, j)),
                  pl.BlockSpec((tf, d), lambda i, j: (j, 0)),
                  pl.BlockSpec((tm, d), lambda i, j: (i, 0)),
                  pl.BlockSpec((1, 1, d), lambda i, j: (i // tps, 0, 0)),
                  pl.BlockSpec(g.shape, lambda i, j: (0, 0))],
        out_specs=pl.BlockSpec((tm, d), lambda i, j: (i, 0)),
        out_shape=jax.ShapeDtypeStruct((n, d), F32),
        scratch_shapes=[pltpu.VMEM((tm, d), F32)],
        compiler_params=_cparams(("arbitrary", "arbitrary")),
        name="ffn_dense",
    )(h, wg, wu, wd, x2, gate, g)


def _ffn_expert_body(blk_e_ref, nvalid_ref, x_ref, wg_ref, wu_ref, wd_ref, o_ref, acc_ref):
    i = pl.program_id(0)
    j = pl.program_id(1)
    valid = i < nvalid_ref[0]

    @pl.when(j == 0)
    def _():
        acc_ref[...] = jnp.zeros(acc_ref.shape, F32)

    @pl.when(valid)
    def _():
        xb = x_ref[...].astype(BF16)
        a = _nn(xb, wg_ref[0].astype(BF16))
        u = _nn(xb, wu_ref[0].astype(BF16))
        acc_ref[...] += _nn((a * _sigmoid(a) * u).astype(BF16), wd_ref[0].astype(BF16))

    @pl.when(j == pl.num_programs(1) - 1)
    def _():
        o_ref[...] = acc_ref[...]


def _ffn_expert(xb, wg, wu, wd, blk_e, nvalid, tmb):
    p, d = xb.shape
    nb = p // tmb
    tf = 512
    nf = D_FF // tf

    def jeff(i, j, nv):
        return jnp.where(i < nv[0], j, nf - 1)

    grid_spec = pltpu.PrefetchScalarGridSpec(
        num_scalar_prefetch=2,
        grid=(nb, nf),
        in_specs=[pl.BlockSpec((tmb, d), lambda i, j, be, nv: (i, 0)),
                  pl.BlockSpec((1, d, tf), lambda i, j, be, nv: (be[i], 0, jeff(i, j, nv))),
                  pl.BlockSpec((1, d, tf), lambda i, j, be, nv: (be[i], 0, jeff(i, j, nv))),
                  pl.BlockSpec((1, tf, d), lambda i, j, be, nv: (be[i], jeff(i, j, nv), 0))],
        out_specs=pl.BlockSpec((tmb, d), lambda i, j, be, nv: (i, 0)),
        scratch_shapes=[pltpu.VMEM((tmb, d), F32)],
    )
    return pl.pallas_call(
        _ffn_expert_body,
        grid_spec=grid_spec,
        out_shape=jax.ShapeDtypeStruct((p, d), F32),
        compiler_params=_cparams(("arbitrary", "arbitrary")),
        name="ffn_expert",
    )(blk_e, nvalid, xb, wg, wu, wd)


DISPATCH_TOKENS = 512


def _moe_dispatch_body(dest_ref, h_ref, xb_in_ref, xb_ref, sem):
    del xb_in_ref

    def row_copy(t, k):
        return pltpu.make_async_copy(h_ref.at[pl.ds(t, 1), :], xb_ref.at[pl.ds(dest_ref[2 * t + k], 1), :], sem)

    def issue(t, c):
        row_copy(t, 0).start(priority=0)
        row_copy(t, 1).start(priority=1)
        return c

    def drain(t, c):
        row_copy(t, 0).wait()
        row_copy(t, 1).wait()
        return c

    lax.fori_loop(0, DISPATCH_TOKENS, issue, 0)
    lax.fori_loop(0, DISPATCH_TOKENS, drain, 0)


def _moe_dispatch(h, dest, p):
    n, d = h.shape
    td = DISPATCH_TOKENS
    return pl.pallas_call(
        _moe_dispatch_body,
        grid=(n // td,),
        in_specs=[pl.BlockSpec((2 * td,), lambda i: (i,), memory_space=pltpu.SMEM),
                  pl.BlockSpec((td, d), lambda i: (i, 0)),
                  pl.BlockSpec(memory_space=pl.ANY)],
        out_specs=pl.BlockSpec(memory_space=pl.ANY),
        out_shape=jax.ShapeDtypeStruct((p, d), h.dtype),
        scratch_shapes=[pltpu.SemaphoreType.DMA(())],
        input_output_aliases={2: 0},
        compiler_params=_cparams(("arbitrary",)),
        name="moe_dispatch",
    )(dest, h, jnp.zeros((p, d), h.dtype))


def _moe_combine_body(y1_ref, y2_ref, rg_ref, x_ref, gate_ref, g_ref, o_ref):
    rg = rg_ref[...]
    y = rg[:, 0:1] * y1_ref[...] + rg[:, 1:2] * y2_ref[...]
    o_ref[...] = x_ref[...] + gate_ref[0] * (_rms(y, D_MODEL) * g_ref[...])


def _moe_combine(y1, y2, rg, x2, gate, g, bsz, seq):
    n = bsz * seq
    d = D_MODEL
    tm = min(512, seq)
    tps = seq // tm
    row = lambda w: pl.BlockSpec((tm, w), lambda r: (r, 0))
    return pl.pallas_call(
        _moe_combine_body,
        grid=(n // tm,),
        in_specs=[row(d), row(d), row(LANES), row(d), pl.BlockSpec((1, 1, d), lambda r: (r // tps, 0, 0)),
                  pl.BlockSpec(g.shape, lambda r: (0, 0))],
        out_specs=row(d),
        out_shape=jax.ShapeDtypeStruct((n, d), F32),
        compiler_params=_cparams(("arbitrary",)),
        name="moe_combine",
    )(y1, y2, rg, x2, gate, g)


def _moe_ffn(h, ri, rg, counts, wg, wu, wd, x2, gate, g, bsz, seq):
    n = bsz * seq
    tmb = min(1024, n // 8)
    cnt = counts[0, :N_EXPERTS].astype(jnp.int32)
    padded = ((cnt + tmb - 1) // tmb) * tmb
    pend = jnp.cumsum(padded)
    pstart = pend - padded
    dest1 = pstart[ri[:, 0]] + ri[:, 2]
    dest2 = pstart[ri[:, 1]] + ri[:, 3]
    nb = (2 * n) // tmb + N_EXPERTS
    p = nb * tmb
    nvalid = (pend[-1] // tmb).astype(jnp.int32).reshape(1)
    blk = jnp.minimum(jnp.arange(nb, dtype=jnp.int32), nvalid[0] - 1) * tmb
    blk_e = jnp.minimum(jnp.sum(blk[:, None] >= pend[None, :], axis=1), N_EXPERTS - 1).astype(jnp.int32)
    xb = _moe_dispatch(h, jnp.stack([dest1, dest2], axis=1).reshape(2 * n), p)
    yb = _ffn_expert(xb, wg, wu, wd, blk_e, nvalid, tmb)
    y1 = jnp.take(yb, dest1, axis=0)
    y2 = jnp.take(yb, dest2, axis=0)
    return _moe_combine(y1, y2, rg, x2, gate, g, bsz, seq)


def _rot_cols(w):
    half = C_ROPE // 2
    return jnp.concatenate([-w[:, half:], w[:, :half]], axis=1)


def _prep_w_in(w_in):
    d = w_in.shape[0]
    z = lambda k: jnp.zeros((d, k), w_in.dtype)
    gates = w_in[:, 2304:2316]
    cq = w_in[:, 2316:2508]
    ckv = w_in[:, 2508:2636]
    kr = w_in[:, 2636:2668]
    cols = [w_in[:, :2304], cq, z(64), ckv,
            gates, z(C_NOPE - 12), kr, z(LANES - C_NOPE - C_ROPE),
            z(C_NOPE), _rot_cols(kr), z(LANES - C_NOPE - C_ROPE)]
    out = jnp.concatenate(cols, axis=1).astype(BF16)
    assert out.shape[1] == IN_COLS
    return out


def _prep_w_uq(w_uq):
    w = w_uq.reshape(C_Q_LORA, C_HEADS, C_NOPE + C_ROPE)
    nope, rp = w[..., :C_NOPE], w[..., C_NOPE:]
    pad = jnp.zeros((C_Q_LORA, C_HEADS, MLA_SLAB - C_NOPE - C_ROPE), w.dtype)
    rot = jnp.concatenate([-rp[..., C_ROPE // 2:], rp[..., :C_ROPE // 2]], axis=-1)
    wq = jnp.concatenate([nope, rp, pad], axis=-1).reshape(C_Q_LORA, MLA_QK_W)
    wqr = jnp.concatenate([jnp.zeros_like(nope), rot, pad], axis=-1).reshape(C_Q_LORA, MLA_QK_W)
    rowpad = jnp.zeros((256 - C_Q_LORA, MLA_QK_W), w.dtype)
    return (jnp.concatenate([wq, rowpad], axis=0).astype(BF16), jnp.concatenate([wqr, rowpad], axis=0).astype(BF16))


def _prep_w_ukv(w_ukv):
    w = w_ukv.reshape(C_KV_LORA, C_HEADS, C_NOPE + C_V_DIM)
    kn = jnp.concatenate([w[..., :C_NOPE], jnp.zeros((C_KV_LORA, C_HEADS, MLA_SLAB - C_NOPE), w.dtype)], axis=-1)
    v = w[..., C_NOPE:]
    return jnp.concatenate([kn.reshape(C_KV_LORA, MLA_QK_W), v.reshape(C_KV_LORA, MLA_V_W)], axis=1).astype(BF16)


def _rope_tables(positions):
    freqs = ROPE_THETA ** (-jnp.arange(0, C_ROPE, 2, dtype=F32) / C_ROPE)
    ang = positions.astype(F32).reshape(-1, 1) * freqs
    n = ang.shape[0]
    cos, sin = jnp.cos(ang), jnp.sin(ang)
    ones = lambda k: jnp.ones((n, k), F32)
    zeros = lambda k: jnp.zeros((n, k), F32)
    tail = MLA_SLAB - C_NOPE - C_ROPE
    return (jnp.concatenate([ones(C_NOPE), cos, cos, ones(tail)], axis=1),
            jnp.concatenate([zeros(C_NOPE), sin, sin, zeros(tail)], axis=1))


def kernel(x, c, positions, w_ada, b_ada, g_norm, w_in, w_out, lam_q1, lam_k1, lam_q2, lam_k2, g_diff, w_conv, b_conv, b_igate, b_fgate, g_mlstm, g_q_lat, g_kv_lat, w_uq, w_ukv, w_gate_d, w_up_d, w_down_d, w_router, b_router, w_gate_e, w_up_e, w_down_e):
    bsz, seq, d = x.shape
    n = bsz * seq
    nc = seq // CHUNK
    depth = w_in.shape[0]
    mods = _ada_all(c, w_ada, b_ada)
    cos_t, sin_t = _rope_tables(positions)
    x2 = x.reshape(n, d)
    row1 = lambda v: v.reshape(1, -1).astype(F32)

    for l in range(depth):
        mod = lambda k, j: mods[2 * l + k, j].reshape(bsz, 1, d)
        wq, wqr = _prep_w_uq(w_uq[l])
        gq = jnp.concatenate([g_q_lat[l], jnp.zeros((256 - C_Q_LORA,), F32)]).reshape(1, 256)
        (qd, kd, vd, qm, km, vm, om, gt, qc, kc, vc) = _mix_in(
            x2, mod(0, 0), mod(0, 1), row1(g_norm[l, 0]), _prep_w_in(w_in[l]), w_conv[l], row1(b_conv[l]),
            gq, row1(g_kv_lat[l]), wq, wqr, _prep_w_ukv(w_ukv[l]), cos_t, sin_t, bsz, seq)

        lam_init = 0.8 - 0.6 * math.exp(-0.3 * l)
        lamv = jnp.zeros((8, LANES), F32).at[0:4, 0:A_QK_DIM].set(jnp.stack([lam_q1[l], lam_k1[l], lam_q2[l], lam_k2[l]]))
        out_a = _diff_attn(qd, kd, vd, lamv, row1(jnp.tile(g_diff[l], A_HEADS)), lam_init, bsz, seq)

        z2 = jnp.zeros((bsz, 2, seq), F32)
        g16 = jnp.concatenate([gt[:, 0:6], z2, gt[:, 6:12], z2], axis=1)
        gates = jnp.pad(g16.reshape(bsz, 16, nc, CHUNK).transpose(0, 2, 1, 3), ((0, 0), (0, 0), (0, 0), (0, LANES - CHUNK)))
        zb = jnp.zeros((2,), F32)
        gbias = jnp.broadcast_to(jnp.concatenate([b_igate[l], zb, b_fgate[l], zb])[:, None], (16, LANES)).astype(F32)
        out_b = _mlstm(qm, km, vm, om, gates, gbias, row1(g_mlstm[l]), bsz, seq)

        out_c = _mla_attn(qc, kc, vc, bsz, seq)

        moe = (l % 2 == 1)
        router = None
        if moe:
            w_hi = w_router[l // 2].astype(BF16)
            w_lo = (w_router[l // 2] - w_hi.astype(F32)).astype(BF16)
            wr = jnp.pad(jnp.concatenate([w_hi, w_lo], axis=1), ((0, 0), (0, LANES - 2 * N_EXPERTS)))
            br = jnp.pad(b_router[l // 2], (0, LANES - N_EXPERTS)).reshape(1, LANES)
            router = (wr, br)
        res = _mix_out(out_a, out_b, out_c, w_out[l].astype(BF16), x2, mod(0, 2), row1(g_norm[l, 1]),
                       mod(1, 0), mod(1, 1), row1(g_norm[l, 2]), router, bsz, seq)
        if moe:
            x2, h2, ri, rg, counts = res
            x2 = _moe_ffn(h2, ri, rg, counts, w_gate_e[l // 2], w_up_e[l // 2], w_down_e[l // 2],
                          x2, mod(1, 2), row1(g_norm[l, 3]), bsz, seq)
        else:
            x2, h2 = res
            x2 = _ffn_dense(h2, w_gate_d[l // 2], w_up_d[l // 2], w_down_d[l // 2],
                            x2, mod(1, 2), row1(g_norm[l, 3]), bsz, seq)
    return x2.reshape(bsz, seq, d)
```

```python
import functools
import math

import jax
import jax.numpy as jnp
import numpy as np
from jax import lax
from jax.experimental import pallas as pl
from jax.experimental.pallas import tpu as pltpu

F32 = jnp.float32
BF16 = jnp.bfloat16
HIGHEST = lax.Precision.HIGHEST

D_MODEL = 1024
DEPTH = 2
CHUNK = 64
EPS = 1e-6
A_HEADS, A_QK_DIM, A_V_DIM = 4, 32, 64
B_HEADS, B_DIM, CONV_WIDTH = 6, 64, 4
C_HEADS, C_Q_LORA, C_KV_LORA, C_NOPE, C_ROPE, C_V_DIM = 6, 192, 128, 64, 32, 64
ROPE_THETA = 10000.0
D_FF = 3584
N_EXPERTS = 8
LOG2E = 1.4426950408889634
NEG_BIG = -1e30

LANES = 128
DIFF_W = 2 * A_HEADS * A_QK_DIM
MLSTM_W = B_HEADS * B_DIM
MLA_SLAB = 128
MLA_QK_W = C_HEADS * MLA_SLAB
MLA_V_W = C_HEADS * C_V_DIM

COL_DIFF = 0
COL_MLSTM = 768
COL_CQ = 2304
COL_CKV = 2560
COL_X = 2688
COL_Y = 2816
IN_COLS = 2944

VMEM_LIMIT = 56 * 1024 * 1024


def _cparams(sem):
    return pltpu.CompilerParams(dimension_semantics=sem, vmem_limit_bytes=VMEM_LIMIT)


def _nt(a, b):
    return lax.dot_general(a, b, (((1,), (1,)), ((), ())), preferred_element_type=F32)


def _nn(a, b):
    return jnp.dot(a, b, preferred_element_type=F32)


def _sigmoid(v):
    return 1.0 / (1.0 + jnp.exp(-v))


def _rms(v, n):
    return v * lax.rsqrt(jnp.sum(v * v, axis=-1, keepdims=True) * (1.0 / n) + EPS)


def _ada_body(c_ref, w_ref, b_ref, o_ref):
    c = c_ref[...]
    sc = c * _sigmoid(c)
    o_ref[0, 0] = jnp.dot(sc, w_ref[0], precision=HIGHEST, preferred_element_type=F32) + b_ref[0]


def _ada_all(c, w_ada, b_ada):
    bsz, d = c.shape
    n = w_ada.shape[0] * w_ada.shape[1]
    w = w_ada.reshape(n, d, 3 * d)
    b = b_ada.reshape(n, 1, 3 * d)
    return pl.pallas_call(
        _ada_body,
        grid=(n, 3),
        in_specs=[pl.BlockSpec((bsz, d), lambda i, j: (0, 0)),
                  pl.BlockSpec((1, d, d), lambda i, j: (i, 0, j)),
                  pl.BlockSpec((1, 1, d), lambda i, j: (i, 0, j))],
        out_specs=pl.BlockSpec((1, 1, bsz, d), lambda i, j: (i, j, 0, 0)),
        out_shape=jax.ShapeDtypeStruct((n, 3, bsz, d), F32),
        compiler_params=_cparams(("arbitrary", "arbitrary")),
        name="ada_mod",
    )(c, w, b)


def _mix_in_body(tiles_per_seq, tm,
                 x_ref, shift_ref, scale_ref, g_ref, win_ref, wconv_ref, bconv_ref, gq_ref, gkv_ref,
                 wq_ref, wqr_ref, wkv_ref, cos_ref, sin_ref,
                 qd_ref, kd_ref, vd_ref, qm_ref, km_ref, vm_ref, om_ref, gt_ref, qc_ref, kc_ref, vc_ref,
                 carry_ref):
    r = pl.program_id(0)
    x = x_ref[...]
    h = _rms(x, D_MODEL) * g_ref[...]
    h = h * (1.0 + scale_ref[0]) + shift_ref[0]
    p = _nn(h.astype(BF16), win_ref[...])

    qd_ref[...] = (p[:, 0:256] * (A_QK_DIM ** -0.5 * LOG2E)).astype(BF16)
    kd_ref[...] = p[:, 256:512].astype(BF16)
    vd_ref[...] = p[:, 512:768].astype(BF16)

    @pl.when(r % tiles_per_seq == 0)
    def _():
        carry_ref[0:8, :] = jnp.zeros((8, 2 * MLSTM_W), F32)

    carry_ref[8:8 + tm, :] = p[:, COL_MLSTM:COL_MLSTM + 2 * MLSTM_W]
    w = wconv_ref[...]
    conv = bconv_ref[...] + carry_ref[8:8 + tm, :] * w[3:4, :]
    for j in range(CONV_WIDTH - 1):
        conv = conv + carry_ref[5 + j:5 + j + tm, :] * w[j:j + 1, :]
    carry_ref[0:8, :] = carry_ref[tm:tm + 8, :]
    qk = conv * _sigmoid(conv)
    qm_ref[...] = qk[:, 0:MLSTM_W].astype(BF16)
    km_ref[...] = (qk[:, MLSTM_W:2 * MLSTM_W] * (B_DIM ** -0.5)).astype(BF16)
    vm_ref[...] = p[:, COL_MLSTM + 2 * MLSTM_W:COL_MLSTM + 3 * MLSTM_W].astype(BF16)
    om_ref[...] = p[:, COL_MLSTM + 3 * MLSTM_W:COL_MLSTM + 4 * MLSTM_W].astype(BF16)

    xg = p[:, COL_X:COL_X + LANES]
    g16 = xg.T[0:16, :]
    lane_g = lax.broadcasted_iota(jnp.int32, (1, LANES), 1)
    for c2 in range(tm // LANES):
        win = g16[:, LANES * c2:LANES * (c2 + 1)]
        gt_ref[0, 2 * c2] = jnp.where(lane_g < CHUNK, win, 0.0)
        gt_ref[0, 2 * c2 + 1] = jnp.where(lane_g < CHUNK, pltpu.roll(win, CHUNK, 1), 0.0)

    cq = p[:, COL_CQ:COL_CQ + 256]
    cqn = (cq * lax.rsqrt(jnp.sum(cq * cq, axis=-1, keepdims=True) * (1.0 / C_Q_LORA) + EPS) * gq_ref[...]).astype(BF16)
    ckv = p[:, COL_CKV:COL_CKV + C_KV_LORA]
    ckvn = (_rms(ckv, C_KV_LORA) * gkv_ref[...]).astype(BF16)
    cos_t = cos_ref[...]
    sin_t = sin_ref[...]
    cos6 = jnp.concatenate([cos_t] * C_HEADS, axis=1)
    sin6 = jnp.concatenate([sin_t] * C_HEADS, axis=1)
    qn = _nn(cqn, wq_ref[...])
    qr = _nn(cqn, wqr_ref[...])
    qc_ref[...] = ((qn * cos6 + qr * sin6) * ((C_NOPE + C_ROPE) ** -0.5 * LOG2E)).astype(BF16)
    kvp = _nn(ckvn, wkv_ref[...])
    lane = lax.broadcasted_iota(jnp.int32, (1, LANES), 1)
    rope_lanes = (lane >= C_NOPE) & (lane < C_NOPE + C_ROPE)
    yg = p[:, COL_Y:COL_Y + LANES]
    krope = jnp.where(rope_lanes, xg * cos_t + yg * sin_t, 0.0)
    kc_ref[...] = (kvp[:, 0:MLA_QK_W] + jnp.concatenate([krope] * C_HEADS, axis=1)).astype(BF16)
    vc_ref[...] = kvp[:, MLA_QK_W:MLA_QK_W + MLA_V_W].astype(BF16)


def _mix_in(x2, shift, scale, g, win, wconv, bconv, gq, gkv, wq, wqr, wkv, cos_t, sin_t, bsz, seq):
    n = bsz * seq
    tm = min(512, seq)
    tps = seq // tm
    d = D_MODEL
    row = lambda w: pl.BlockSpec((tm, w), lambda r: (r, 0))
    full = lambda a: pl.BlockSpec(a.shape, lambda r: (0,) * a.ndim)
    mod = pl.BlockSpec((1, 1, d), lambda r: (r // tps, 0, 0))
    out_widths = [DIFF_W, DIFF_W, DIFF_W, MLSTM_W, MLSTM_W, MLSTM_W, MLSTM_W]
    out_shape = [jax.ShapeDtypeStruct((n, w), BF16) for w in out_widths]
    out_specs = [row(w) for w in out_widths]
    out_shape.append(jax.ShapeDtypeStruct((bsz, seq // CHUNK, 16, LANES), F32))
    out_specs.append(pl.BlockSpec((1, tm // CHUNK, 16, LANES), lambda r: (r // tps, r % tps, 0, 0)))
    for w in (MLA_QK_W, MLA_QK_W, MLA_V_W):
        out_shape.append(jax.ShapeDtypeStruct((n, w), BF16))
        out_specs.append(row(w))
    return pl.pallas_call(
        functools.partial(_mix_in_body, tps, tm),
        grid=(n // tm,),
        in_specs=[row(d), mod, mod, full(g), full(win), full(wconv), full(bconv), full(gq), full(gkv),
                  full(wq), full(wqr), full(wkv), row(LANES), row(LANES)],
        out_specs=out_specs,
        out_shape=out_shape,
        scratch_shapes=[pltpu.VMEM((tm + 8, 2 * MLSTM_W), F32)],
        compiler_params=_cparams(("arbitrary",)),
        name="mix_in",
    )(x2, shift, scale, g, win, wconv, bconv, gq, gkv, wq, wqr, wkv, cos_t, sin_t)


def _chunk_mask(tq, tk):
    ri = lax.broadcasted_iota(jnp.int32, (tq, tk), 0)
    ci = lax.broadcasted_iota(jnp.int32, (tq, tk), 1)
    return (ci >> 6) <= (ri >> 6)


def _two_pass_softmax(i, nq, tk, score_fn, pv_fn, mask, s_ref, p_ref, mx_ref, l_ref):
    rows = mx_ref.shape[0]
    tq = mask.shape[0]
    mx_ref[...] = jnp.full(mx_ref.shape, NEG_BIG, F32)

    def pass1(j, masked):
        s = score_fn(j)
        if masked:
            s = jnp.where(mask, s.reshape(rows // tq, tq, tk), NEG_BIG).reshape(rows, tk)
        s_ref[j] = s
        mx_ref[...] = jnp.maximum(mx_ref[...], jnp.maximum(s[:, 0:LANES], s[:, LANES:2 * LANES]))

    def body1(j, c):
        pass1(j, False)
        return c

    lax.fori_loop(0, i, body1, 0)
    pass1(i, True)
    mx_ref[...] = jnp.broadcast_to(jnp.max(mx_ref[...], axis=1, keepdims=True), mx_ref.shape)

    rb = 128

    def pass2_pv(n):
        for j in range(n):
            for r0 in range(0, rows, rb):
                s = s_ref[j, r0:r0 + rb, :]
                m = mx_ref[r0:r0 + rb, :]
                p0 = jnp.exp2(s[:, 0:LANES] - m)
                p1 = jnp.exp2(s[:, LANES:2 * LANES] - m)
                l_ref[r0:r0 + rb, :] = (p0 + p1) if j == 0 else (l_ref[r0:r0 + rb, :] + (p0 + p1))
                p_ref[j, r0:r0 + rb, :] = jnp.concatenate([p0, p1], axis=1).astype(BF16)
        pv_fn(n)

    for n in range(1, nq + 1):
        @pl.when(i == n - 1)
        def _():
            pass2_pv(n)
    return jnp.sum(l_ref[...], axis=1, keepdims=True)


def _diff_attn_body(tq, nq, lam_init, q_ref, k_ref, v_ref, lam_ref, g_ref, o_ref,
                    qs_ref, s_ref, p_ref, mx_ref, l_ref, acc_ref):
    i = pl.program_id(1)
    tk = tq
    nslab = 2 * A_HEADS
    q = q_ref[...]
    lane = lax.broadcasted_iota(jnp.int32, (1, DIFF_W), 1)
    zero = jnp.zeros_like(q)
    for s in range(nslab):
        qs_ref[s * tq:(s + 1) * tq, :] = jnp.where((lane >= A_QK_DIM * s) & (lane < A_QK_DIM * (s + 1)), q, zero)
    mask = _chunk_mask(tq, tk)

    def score(j):
        return _nt(qs_ref[...], k_ref[pl.ds(pl.multiple_of(j * tk, tk), tk), :])

    def pv(n):
        acc_ref[...] = sum(_nn(p_ref[j], v_ref[j * tk:(j + 1) * tk, :]) for j in range(n))

    l = _two_pass_softmax(i, nq, tk, score, pv, mask, s_ref, p_ref, mx_ref, l_ref)

    lv = lam_ref[...]
    lam = (jnp.exp(jnp.sum(lv[0:1] * lv[1:2], axis=1, keepdims=True))
           - jnp.exp(jnp.sum(lv[2:3] * lv[3:4], axis=1, keepdims=True)) + lam_init)
    o = acc_ref[...] / l
    out = jnp.zeros((tq, DIFF_W), F32)
    for hd in range(A_HEADS):
        sel = (lane >= A_V_DIM * hd) & (lane < A_V_DIM * (hd + 1))
        oh = o[hd * tq:(hd + 1) * tq] - lam * o[(A_HEADS + hd) * tq:(A_HEADS + hd + 1) * tq]
        ms = jnp.sum(jnp.where(sel, oh * oh, 0.0), axis=1, keepdims=True) * (1.0 / A_V_DIM)
        out = out + jnp.where(sel, oh * lax.rsqrt(ms + EPS), 0.0)
    o_ref[...] = (out * g_ref[...] * (1.0 - lam_init)).astype(BF16)


def _diff_attn(qd, kd, vd, lamv, g4, lam_init, bsz, seq):
    tq = min(256, seq)
    nq = seq // tq
    nslab = 2 * A_HEADS
    rows = nslab * tq
    return pl.pallas_call(
        functools.partial(_diff_attn_body, tq, nq, lam_init),
        grid=(bsz, nq),
        in_specs=[pl.BlockSpec((tq, DIFF_W), lambda b, i: (b * nq + i, 0)),
                  pl.BlockSpec((seq, DIFF_W), lambda b, i: (b, 0)),
                  pl.BlockSpec((seq, DIFF_W), lambda b, i: (b, 0)),
                  pl.BlockSpec(lamv.shape, lambda b, i: (0, 0)),
                  pl.BlockSpec(g4.shape, lambda b, i: (0, 0))],
        out_specs=pl.BlockSpec((tq, DIFF_W), lambda b, i: (b * nq + i, 0)),
        out_shape=jax.ShapeDtypeStruct((bsz * seq, DIFF_W), BF16),
        scratch_shapes=[pltpu.VMEM((rows, DIFF_W), BF16), pltpu.VMEM((nq, rows, tq), F32),
                        pltpu.VMEM((nq, rows, tq), BF16), pltpu.VMEM((rows, LANES), F32),
                        pltpu.VMEM((rows, LANES), F32), pltpu.VMEM((rows, DIFF_W), F32)],
        compiler_params=_cparams(("arbitrary", "arbitrary")),
        name="diff_attn",
    )(qd, kd, vd, lamv, g4)


def _mla_attn_body(tq, nq, q_ref, k_ref, v_ref, o_ref, s_ref, p_ref, mx_ref, l_ref, acc1_ref, acc2_ref):
    i = pl.program_id(1)
    tk = tq
    mask = _chunk_mask(tq, tk)
    g1 = 4

    def score(j):
        kt = k_ref[pl.ds(pl.multiple_of(j * tk, tk), tk), :]
        return jnp.concatenate([_nt(q_ref[:, MLA_SLAB * hd:MLA_SLAB * (hd + 1)], kt[:, MLA_SLAB * hd:MLA_SLAB * (hd + 1)])
                                for hd in range(C_HEADS)], axis=0)

    def pv(n):
        acc1_ref[...] = sum(_nn(p_ref[j, 0:g1 * tq, :], v_ref[j * tk:(j + 1) * tk, 0:256]) for j in range(n))
        acc2_ref[...] = sum(_nn(p_ref[j, g1 * tq:, :], v_ref[j * tk:(j + 1) * tk, 256:MLA_V_W]) for j in range(n))

    l = _two_pass_softmax(i, nq, tk, score, pv, mask, s_ref, p_ref, mx_ref, l_ref)

    linv = 1.0 / l
    o1 = acc1_ref[...] * linv[0:g1 * tq]
    o2 = acc2_ref[...] * linv[g1 * tq:]
    lane1 = lax.broadcasted_iota(jnp.int32, (1, 256), 1)
    lane2 = lax.broadcasted_iota(jnp.int32, (1, 128), 1)
    out1 = jnp.zeros((tq, 256), F32)
    for hd in range(g1):
        out1 = out1 + jnp.where((lane1 >= C_V_DIM * hd) & (lane1 < C_V_DIM * (hd + 1)), o1[hd * tq:(hd + 1) * tq], 0.0)
    out2 = jnp.zeros((tq, 128), F32)
    for hd in range(C_HEADS - g1):
        out2 = out2 + jnp.where((lane2 >= C_V_DIM * hd) & (lane2 < C_V_DIM * (hd + 1)), o2[hd * tq:(hd + 1) * tq], 0.0)
    o_ref[...] = jnp.concatenate([out1, out2], axis=1).astype(BF16)


def _mla_attn(qc, kc, vc, bsz, seq):
    tq = min(256, seq)
    nq = seq // tq
    return pl.pallas_call(
        functools.partial(_mla_attn_body, tq, nq),
        grid=(bsz, nq),
        in_specs=[pl.BlockSpec((tq, MLA_QK_W), lambda b, i: (b * nq + i, 0)),
                  pl.BlockSpec((seq, MLA_QK_W), lambda b, i: (b, 0)),
                  pl.BlockSpec((seq, MLA_V_W), lambda b, i: (b, 0))],
        out_specs=pl.BlockSpec((tq, MLA_V_W), lambda b, i: (b * nq + i, 0)),
        out_shape=jax.ShapeDtypeStruct((bsz * seq, MLA_V_W), BF16),
        scratch_shapes=[pltpu.VMEM((nq, C_HEADS * tq, tq), F32), pltpu.VMEM((nq, C_HEADS * tq, tq), BF16),
                        pltpu.VMEM((C_HEADS * tq, LANES), F32), pltpu.VMEM((C_HEADS * tq, LANES), F32),
                        pltpu.VMEM((4 * tq, 256), F32), pltpu.VMEM((2 * tq, 128), F32)],
        compiler_params=_cparams(("arbitrary", "arbitrary")),
        name="mla_attn",
    )(qc, kc, vc)


def _mlstm_body(nc, q_ref, k_ref, v_ref, og_ref, gates_ref, gbias_ref, gout_ref, o_ref,
                ct_ref, m_ref, a_ref, b_ref, al_ref, bl_ref):
    ct_ref[...] = jnp.zeros(ct_ref.shape, F32)
    m_ref[...] = jnp.zeros(m_ref.shape, F32)
    lane = lax.broadcasted_iota(jnp.int32, (1, LANES), 1)
    ti = lax.broadcasted_iota(jnp.int32, (CHUNK, CHUNK), 0)
    si = lax.broadcasted_iota(jnp.int32, (CHUNK, CHUNK), 1)
    tril = si <= ti
    eye = si == ti
    eye128 = (lax.broadcasted_iota(jnp.int32, (LANES, LANES), 0)
              == lax.broadcasted_iota(jnp.int32, (LANES, LANES), 1)).astype(BF16)
    gbias = gbias_ref[...]
    gout = gout_ref[...]

    g = gates_ref[0]
    ig = (g[:, 0:8, :] + gbias[0:8]).reshape(nc * 8, LANES)
    fg = (g[:, 8:16, :] + gbias[8:16]).reshape(nc * 8, LANES)
    lf = jnp.minimum(fg, 0.0) - jnp.log(1.0 + jnp.exp(-jnp.abs(fg)))
    b_all = jnp.where(lane < CHUNK, lf, 0.0)
    for sh in (1, 2, 4, 8, 16, 32):
        b_all = b_all + jnp.where(lane >= sh, pltpu.roll(b_all, sh, 1), 0.0)
    a_all = ig - b_all
    a_ref[...] = a_all.reshape(nc, 8, LANES)
    b_ref[...] = b_all.reshape(nc, 8, LANES)
    al_ref[...] = jnp.broadcast_to(jnp.max(jnp.where(lane < CHUNK, a_all, NEG_BIG), axis=1, keepdims=True),
                                   (nc * 8, LANES)).reshape(nc, 8, LANES)
    bl_ref[...] = jnp.broadcast_to(jnp.sum(jnp.where(lane == CHUNK - 1, b_all, 0.0), axis=1, keepdims=True),
                                   (nc * 8, LANES)).reshape(nc, 8, LANES)

    def chunk(c, carry):
        rows = pl.ds(pl.multiple_of(c * CHUNK, CHUNK), CHUNK)
        a = a_ref[c]
        b = b_ref[c]
        b_last = bl_ref[c]

        for pr in range(B_HEADS // 2):
            cols = slice(LANES * pr, LANES * (pr + 1))
            qw = q_ref[rows, cols]
            kw = k_ref[rows, cols]
            vw = v_ref[rows, cols]
            kt = _nt(eye128, kw)
            ctp = ct_ref[pr]
            ctb = ctp.astype(BF16)
            outs = []
            for hh in range(2):
                hd = 2 * pr + hh
                half = (lane < CHUNK) if hh == 0 else (lane >= CHUNK)
                onecol = (lane == CHUNK) if hh == 0 else (lane == 0)
                qmk = jnp.where(half, qw, jnp.zeros_like(qw))
                vaug = jnp.where(half, vw, jnp.where(onecol, 1.0, 0.0).astype(BF16))
                a_h = a[hd:hd + 1, 0:CHUNK]
                b_h = b[hd:hd + 1, 0:CHUNK]
                full = (CHUNK, LANES)
                a_col = jnp.broadcast_to(jnp.max(jnp.where(tril, a_h, NEG_BIG), axis=1, keepdims=True), full)
                b_col = jnp.broadcast_to(jnp.sum(jnp.where(eye, b_h, 0.0), axis=1, keepdims=True), full)
                w0 = jnp.where(tril, jnp.exp(a_h - a_col[:, 0:CHUNK]), 0.0)
                sc0 = (_nt(qmk, kw) * w0).astype(BF16)
                x0 = _nn(sc0, vaug)
                xi = _nn(qmk, ctb)
                m = m_ref[hd:hd + 1, :]
                m_col = jnp.maximum(a_col, m)
                xs = jnp.exp(a_col - m_col) * x0 + jnp.exp(m - m_col) * xi
                den = jnp.broadcast_to(jnp.sum(jnp.where(onecol, xs, 0.0), axis=1, keepdims=True), full)
                hv = xs / jnp.maximum(jnp.abs(den), jnp.exp(-b_col - m_col))
                ms = jnp.broadcast_to(jnp.sum(jnp.where(half, hv * hv, 0.0), axis=1, keepdims=True), full)
                outs.append(hv * lax.rsqrt(ms * (1.0 / B_DIM) + EPS))
                al = al_ref[c][hd:hd + 1, :]
                m_l = jnp.maximum(al, m)
                kwt = (kt[CHUNK * hh:CHUNK * (hh + 1), :] * jnp.exp(a_h - al[:, 0:CHUNK])).astype(BF16)
                dct = _nn(kwt, vaug)
                ct_ref[pr, CHUNK * hh:CHUNK * (hh + 1), :] = (jnp.exp(m - m_l) * ctp[CHUNK * hh:CHUNK * (hh + 1)]
                                                              + jnp.exp(al - m_l) * dct)
                m_ref[hd:hd + 1, :] = b_last[hd:hd + 1, :] + m_l
            hn = jnp.where(lane < CHUNK, outs[0], outs[1])
            og = og_ref[rows, cols].astype(F32)
            o_ref[rows, cols] = (hn * gout[:, cols] * _sigmoid(og)).astype(BF16)
        return carry

    lax.fori_loop(0, nc, chunk, 0, unroll=2)


def _mlstm(qm, km, vm, om, gates, gbias, gout, bsz, seq):
    nc = seq // CHUNK
    blk = pl.BlockSpec((seq, MLSTM_W), lambda b: (b, 0))
    return pl.pallas_call(
        functools.partial(_mlstm_body, nc),
        grid=(bsz,),
        in_specs=[blk, blk, blk, blk,
                  pl.BlockSpec((1, nc, 16, LANES), lambda b: (b, 0, 0, 0)),
                  pl.BlockSpec(gbias.shape, lambda b: (0, 0)),
                  pl.BlockSpec(gout.shape, lambda b: (0, 0))],
        out_specs=blk,
        out_shape=jax.ShapeDtypeStruct((bsz * seq, MLSTM_W), BF16),
        scratch_shapes=[pltpu.VMEM((B_HEADS // 2, LANES, LANES), F32), pltpu.VMEM((8, LANES), F32)]
        + [pltpu.VMEM((nc, 8, LANES), F32)] * 4,
        compiler_params=_cparams(("arbitrary",)),
        name="mlstm",
    )(qm, km, vm, om, gates, gbias, gout)


def _mix_out_body(with_router, tm,
                  oa_ref, ob_ref, oc_ref, wout_ref, x_ref, gate_ref, g1_ref, shift_ref, scale_ref, g2_ref, *rest):
    if with_router:
        wr_ref, br_ref, xo_ref, h_ref, ri_ref, rg_ref, cnt_ref, run_ref = rest
    else:
        xo_ref, h_ref = rest
    na = A_HEADS * A_V_DIM
    y = (_nn(oa_ref[...], wout_ref[0:na, :]) + _nn(ob_ref[...], wout_ref[na:na + MLSTM_W, :])
         + _nn(oc_ref[...], wout_ref[na + MLSTM_W:, :]))
    xn = x_ref[...] + gate_ref[0] * (_rms(y, D_MODEL) * g1_ref[...])
    xo_ref[...] = xn
    h = _rms(xn, D_MODEL) * g2_ref[...]
    h = h * (1.0 + scale_ref[0]) + shift_ref[0]
    h_ref[...] = h.astype(h_ref.dtype)
    if with_router:
        @pl.when(pl.program_id(0) == 0)
        def _():
            run_ref[...] = jnp.zeros(run_ref.shape, F32)

        lane = lax.broadcasted_iota(jnp.int32, (1, LANES), 1)
        h_hi = h.astype(BF16)
        h_lo = (h - h_hi.astype(F32)).astype(BF16)
        part = _nn(h_hi, wr_ref[...]) + _nn(h_lo, wr_ref[...])
        logits = part + pltpu.roll(part, LANES - N_EXPERTS, 1) + br_ref[...]
        lg = jnp.where(lane < N_EXPERTS, logits, NEG_BIG)
        v1 = jnp.max(lg, axis=1, keepdims=True)
        i1 = jnp.min(jnp.where(lg == v1, lane, LANES), axis=1, keepdims=True)
        lg2 = jnp.where(lane == i1, NEG_BIG, lg)
        v2 = jnp.max(lg2, axis=1, keepdims=True)
        i2 = jnp.min(jnp.where(lg2 == v2, lane, LANES), axis=1, keepdims=True)
        gt1 = 1.0 / (1.0 + jnp.exp(v2 - v1))
        oh1 = lane == i1
        oh2 = lane == i2
        cnt = jnp.where(oh1 | oh2, 1.0, 0.0)
        ri_ = lax.broadcasted_iota(jnp.int32, (tm, tm), 0)
        ci_ = lax.broadcasted_iota(jnp.int32, (tm, tm), 1)
        lower = jnp.where(ci_ < ri_, 1.0, 0.0).astype(BF16)
        before = _nn(lower, cnt.astype(BF16)) + run_ref[0:1, :]
        r1 = jnp.sum(jnp.where(oh1, before, 0.0), axis=1, keepdims=True)
        r2 = jnp.sum(jnp.where(oh2, before, 0.0), axis=1, keepdims=True)
        run_ref[...] = run_ref[...] + jnp.sum(cnt, axis=0, keepdims=True)
        ri_ref[...] = jnp.where(lane == 0, i1, jnp.where(lane == 1, i2, jnp.where(
            lane == 2, r1.astype(jnp.int32), jnp.where(lane == 3, r2.astype(jnp.int32), 0))))
        rg_ref[...] = jnp.where(lane == 0, gt1, jnp.where(lane == 1, 1.0 - gt1, 0.0))
        cnt_ref[...] = run_ref[...]


def _mix_out(oa, ob, oc, wout, x2, gate, g1, shift, scale, g2, router, bsz, seq):
    n = bsz * seq
    d = D_MODEL
    tm = min(512, seq)
    tps = seq // tm
    row = lambda w: pl.BlockSpec((tm, w), lambda r: (r, 0))
    full = lambda a: pl.BlockSpec(a.shape, lambda r: (0,) * a.ndim)
    mod = pl.BlockSpec((1, 1, d), lambda r: (r // tps, 0, 0))
    in_specs = [row(oa.shape[1]), row(ob.shape[1]), row(oc.shape[1]), full(wout), row(d), mod, full(g1), mod, mod, full(g2)]
    args = [oa, ob, oc, wout, x2, gate, g1, shift, scale, g2]
    out_shape = [jax.ShapeDtypeStruct((n, d), F32), jax.ShapeDtypeStruct((n, d), BF16 if router is None else F32)]
    out_specs = [row(d), row(d)]
    scratch = []
    if router is not None:
        wr, br = router
        in_specs += [full(wr), full(br)]
        args += [wr, br]
        out_shape += [jax.ShapeDtypeStruct((n, LANES), jnp.int32), jax.ShapeDtypeStruct((n, LANES), F32),
                      jax.ShapeDtypeStruct((8, LANES), F32)]
        out_specs += [row(LANES), row(LANES), pl.BlockSpec((8, LANES), lambda r: (0, 0))]
        scratch = [pltpu.VMEM((8, LANES), F32)]
    return pl.pallas_call(
        functools.partial(_mix_out_body, router is not None, tm),
        grid=(n // tm,),
        in_specs=in_specs,
        out_specs=out_specs,
        out_shape=out_shape,
        scratch_shapes=scratch,
        compiler_params=_cparams(("arbitrary",)),
        name="mix_out_router" if router is not None else "mix_out",
    )(*args)


def _ffn_dense_body(h_ref, wg_ref, wu_ref, wd_ref, x_ref, gate_ref, g_ref, o_ref, acc_ref):
    j = pl.program_id(1)

    @pl.when(j == 0)
    def _():
        acc_ref[...] = jnp.zeros(acc_ref.shape, F32)

    h = h_ref[...]
    a = _nn(h, wg_ref[...].astype(BF16))
    u = _nn(h, wu_ref[...].astype(BF16))
    acc_ref[...] += _nn((a * _sigmoid(a) * u).astype(BF16), wd_ref[...].astype(BF16))

    @pl.when(j == pl.num_programs(1) - 1)
    def _():
        o_ref[...] = x_ref[...] + gate_ref[0] * (_rms(acc_ref[...], D_MODEL) * g_ref[...])


def _ffn_dense(h, wg, wu, wd, x2, gate, g, bsz, seq):
    n = bsz * seq
    d = D_MODEL
    tm = min(1024, seq)
    tps = seq // tm
    tf = 512
    return pl.pallas_call(
        _ffn_dense_body,
        grid=(n // tm, D_FF // tf),
        in_specs=[pl.BlockSpec((tm, d), lambda i, j: (i, 0)),
                  pl.BlockSpec((d, tf), lambda i, j: (0, j)),
                  pl.BlockSpec((d, tf), lambda i, j: (0, j)),
                  pl.BlockSpec((tf, d), lambda i, j: (j, 0)),
                  pl.BlockSpec((tm, d), lambda i, j: (i, 0)),
                  pl.BlockSpec((1, 1, d), lambda i, j: (i // tps, 0, 0)),
                  pl.BlockSpec(g.shape, lambda i, j: (0, 0))],
        out_specs=pl.BlockSpec((tm, d), lambda i, j: (i, 0)),
        out_shape=jax.ShapeDtypeStruct((n, d), F32),
        scratch_shapes=[pltpu.VMEM((tm, d), F32)],
        compiler_params=_cparams(("arbitrary", "arbitrary")),
        name="ffn_dense",
    )(h, wg, wu, wd, x2, gate, g)


def _ffn_expert_body(blk_e_ref, nvalid_ref, x_ref, wg_ref, wu_ref, wd_ref, o_ref, acc_ref):
    i = pl.program_id(0)
    j = pl.program_id(1)
    valid = i < nvalid_ref[0]

    @pl.when(j == 0)
    def _():
        acc_ref[...] = jnp.zeros(acc_ref.shape, F32)

    @pl.when(valid)
    def _():
        xb = x_ref[...].astype(BF16)
        a = _nn(xb, wg_ref[0].astype(BF16))
        u = _nn(xb, wu_ref[0].astype(BF16))
        acc_ref[...] += _nn((a * _sigmoid(a) * u).astype(BF16), wd_ref[0].astype(BF16))

    @pl.when(j == pl.num_programs(1) - 1)
    def _():
        o_ref[...] = acc_ref[...]


def _ffn_expert(xb, wg, wu, wd, blk_e, nvalid, tmb):
    p, d = xb.shape
    nb = p // tmb
    tf = 512
    nf = D_FF // tf

    def jeff(i, j, nv):
        return jnp.where(i < nv[0], j, nf - 1)

    grid_spec = pltpu.PrefetchScalarGridSpec(
        num_scalar_prefetch=2,
        grid=(nb, nf),
        in_specs=[pl.BlockSpec((tmb, d), lambda i, j, be, nv: (i, 0)),
                  pl.BlockSpec((1, d, tf), lambda i, j, be, nv: (be[i], 0, jeff(i, j, nv))),
                  pl.BlockSpec((1, d, tf), lambda i, j, be, nv: (be[i], 0, jeff(i, j, nv))),
                  pl.BlockSpec((1, tf, d), lambda i, j, be, nv: (be[i], jeff(i, j, nv), 0))],
        out_specs=pl.BlockSpec((tmb, d), lambda i, j, be, nv: (i, 0)),
        scratch_shapes=[pltpu.VMEM((tmb, d), F32)],
    )
    return pl.pallas_call(
        _ffn_expert_body,
        grid_spec=grid_spec,
        out_shape=jax.ShapeDtypeStruct((p, d), F32),
        compiler_params=_cparams(("arbitrary", "arbitrary")),
        name="ffn_expert",
    )(blk_e, nvalid, xb, wg, wu, wd)


DISPATCH_TOKENS = 512


def _moe_dispatch_body(dest_ref, h_ref, xb_in_ref, xb_ref, sem):
    del xb_in_ref

    def row_copy(t, k):
        return pltpu.make_async_copy(h_ref.at[pl.ds(t, 1), :], xb_ref.at[pl.ds(dest_ref[2 * t + k], 1), :], sem)

    def issue(t, c):
        row_copy(t, 0).start(priority=0)
        row_copy(t, 1).start(priority=1)
        return c

    def drain(t, c):
        row_copy(t, 0).wait()
        row_copy(t, 1).wait()
        return c

    lax.fori_loop(0, DISPATCH_TOKENS, issue, 0)
    lax.fori_loop(0, DISPATCH_TOKENS, drain, 0)


def _moe_dispatch(h, dest, p):
    n, d = h.shape
    td = DISPATCH_TOKENS
    return pl.pallas_call(
        _moe_dispatch_body,
        grid=(n // td,),
        in_specs=[pl.BlockSpec((2 * td,), lambda i: (i,), memory_space=pltpu.SMEM),
                  pl.BlockSpec((td, d), lambda i: (i, 0)),
                  pl.BlockSpec(memory_space=pl.ANY)],
        out_specs=pl.BlockSpec(memory_space=pl.ANY),
        out_shape=jax.ShapeDtypeStruct((p, d), h.dtype),
        scratch_shapes=[pltpu.SemaphoreType.DMA(())],
        input_output_aliases={2: 0},
        compiler_params=_cparams(("arbitrary",)),
        name="moe_dispatch",
    )(dest, h, jnp.zeros((p, d), h.dtype))


def _moe_combine_body(y1_ref, y2_ref, rg_ref, x_ref, gate_ref, g_ref, o_ref):
    rg = rg_ref[...]
    y = rg[:, 0:1] * y1_ref[...] + rg[:, 1:2] * y2_ref[...]
    o_ref[...] = x_ref[...] + gate_ref[0] * (_rms(y, D_MODEL) * g_ref[...])


def _moe_combine(y1, y2, rg, x2, gate, g, bsz, seq):
    n = bsz * seq
    d = D_MODEL
    tm = min(512, seq)
    tps = seq // tm
    row = lambda w: pl.BlockSpec((tm, w), lambda r: (r, 0))
    return pl.pallas_call(
        _moe_combine_body,
        grid=(n // tm,),
        in_specs=[row(d), row(d), row(LANES), row(d), pl.BlockSpec((1, 1, d), lambda r: (r // tps, 0, 0)),
                  pl.BlockSpec(g.shape, lambda r: (0, 0))],
        out_specs=row(d),
        out_shape=jax.ShapeDtypeStruct((n, d), F32),
        compiler_params=_cparams(("arbitrary",)),
        name="moe_combine",
    )(y1, y2, rg, x2, gate, g)


def _moe_ffn(h, ri, rg, counts, wg, wu, wd, x2, gate, g, bsz, seq):
    n = bsz * seq
    tmb = min(1024, n // 8)
    cnt = counts[0, :N_EXPERTS].astype(jnp.int32)
    padded = ((cnt + tmb - 1) // tmb) * tmb
    pend = jnp.cumsum(padded)
    pstart = pend - padded
    dest1 = pstart[ri[:, 0]] + ri[:, 2]
    dest2 = pstart[ri[:, 1]] + ri[:, 3]
    nb = (2 * n) // tmb + N_EXPERTS
    p = nb * tmb
    nvalid = (pend[-1] // tmb).astype(jnp.int32).reshape(1)
    blk = jnp.minimum(jnp.arange(nb, dtype=jnp.int32), nvalid[0] - 1) * tmb
    blk_e = jnp.minimum(jnp.sum(blk[:, None] >= pend[None, :], axis=1), N_EXPERTS - 1).astype(jnp.int32)
    xb = _moe_dispatch(h, jnp.stack([dest1, dest2], axis=1).reshape(2 * n), p)
    yb = _ffn_expert(xb, wg, wu, wd, blk_e, nvalid, tmb)
    y1 = jnp.take(yb, dest1, axis=0)
    y2 = jnp.take(yb, dest2, axis=0)
    return _moe_combine(y1, y2, rg, x2, gate, g, bsz, seq)


def _rot_cols(w):
    half = C_ROPE // 2
    return jnp.concatenate([-w[:, half:], w[:, :half]], axis=1)


def _prep_w_in(w_in):
    d = w_in.shape[0]
    z = lambda k: jnp.zeros((d, k), w_in.dtype)
    gates = w_in[:, 2304:2316]
    cq = w_in[:, 2316:2508]
    ckv = w_in[:, 2508:2636]
    kr = w_in[:, 2636:2668]
    cols = [w_in[:, :2304], cq, z(64), ckv,
            gates[:, 0:B_HEADS], z(2), gates[:, B_HEADS:], z(C_NOPE - 2 * B_HEADS - 2), kr, z(LANES - C_NOPE - C_ROPE),
            z(C_NOPE), _rot_cols(kr), z(LANES - C_NOPE - C_ROPE)]
    out = jnp.concatenate(cols, axis=1).astype(BF16)
    assert out.shape[1] == IN_COLS
    return out


def _prep_w_uq(w_uq):
    w = w_uq.reshape(C_Q_LORA, C_HEADS, C_NOPE + C_ROPE)
    nope, rp = w[..., :C_NOPE], w[..., C_NOPE:]
    pad = jnp.zeros((C_Q_LORA, C_HEADS, MLA_SLAB - C_NOPE - C_ROPE), w.dtype)
    rot = jnp.concatenate([-rp[..., C_ROPE // 2:], rp[..., :C_ROPE // 2]], axis=-1)
    wq = jnp.concatenate([nope, rp, pad], axis=-1).reshape(C_Q_LORA, MLA_QK_W)
    wqr = jnp.concatenate([jnp.zeros_like(nope), rot, pad], axis=-1).reshape(C_Q_LORA, MLA_QK_W)
    rowpad = jnp.zeros((256 - C_Q_LORA, MLA_QK_W), w.dtype)
    return (jnp.concatenate([wq, rowpad], axis=0).astype(BF16), jnp.concatenate([wqr, rowpad], axis=0).astype(BF16))


def _prep_w_ukv(w_ukv):
    w = w_ukv.reshape(C_KV_LORA, C_HEADS, C_NOPE + C_V_DIM)
    kn = jnp.concatenate([w[..., :C_NOPE], jnp.zeros((C_KV_LORA, C_HEADS, MLA_SLAB - C_NOPE), w.dtype)], axis=-1)
    v = w[..., C_NOPE:]
    return jnp.concatenate([kn.reshape(C_KV_LORA, MLA_QK_W), v.reshape(C_KV_LORA, MLA_V_W)], axis=1).astype(BF16)


def _rope_tables(positions):
    freqs = ROPE_THETA ** (-jnp.arange(0, C_ROPE, 2, dtype=F32) / C_ROPE)
    tail = MLA_SLAB - C_NOPE - C_ROPE
    f128 = jnp.concatenate([jnp.zeros((C_NOPE,), F32), freqs, freqs, jnp.zeros((tail,), F32)])
    ang = positions.astype(F32).reshape(-1, 1) * f128[None, :]
    return jnp.cos(ang), jnp.sin(ang)


def kernel(x, c, positions, w_ada, b_ada, g_norm, w_in, w_out, lam_q1, lam_k1, lam_q2, lam_k2, g_diff, w_conv, b_conv, b_igate, b_fgate, g_mlstm, g_q_lat, g_kv_lat, w_uq, w_ukv, w_gate_d, w_up_d, w_down_d, w_router, b_router, w_gate_e, w_up_e, w_down_e):
    bsz, seq, d = x.shape
    n = bsz * seq
    nc = seq // CHUNK
    depth = w_in.shape[0]
    mods = _ada_all(c, w_ada, b_ada)
    cos_t, sin_t = _rope_tables(positions)
    x2 = x.reshape(n, d)
    row1 = lambda v: v.reshape(1, -1).astype(F32)

    for l in range(depth):
        mod = lambda k, j: mods[2 * l + k, j].reshape(bsz, 1, d)
        wq, wqr = _prep_w_uq(w_uq[l])
        gq = jnp.concatenate([g_q_lat[l], jnp.zeros((256 - C_Q_LORA,), F32)]).reshape(1, 256)
        (qd, kd, vd, qm, km, vm, om, gt, qc, kc, vc) = _mix_in(
            x2, mod(0, 0), mod(0, 1), row1(g_norm[l, 0]), _prep_w_in(w_in[l]), w_conv[l], row1(b_conv[l]),
            gq, row1(g_kv_lat[l]), wq, wqr, _prep_w_ukv(w_ukv[l]), cos_t, sin_t, bsz, seq)

        lam_init = 0.8 - 0.6 * math.exp(-0.3 * l)
        lamv = jnp.zeros((8, LANES), F32).at[0:4, 0:A_QK_DIM].set(jnp.stack([lam_q1[l], lam_k1[l], lam_q2[l], lam_k2[l]]))
        out_a = _diff_attn(qd, kd, vd, lamv, row1(jnp.tile(g_diff[l], A_HEADS)), lam_init, bsz, seq)

        gates = gt
        zb = jnp.zeros((2,), F32)
        gbias = jnp.broadcast_to(jnp.concatenate([b_igate[l], zb, b_fgate[l], zb])[:, None], (16, LANES)).astype(F32)
        out_b = _mlstm(qm, km, vm, om, gates, gbias, row1(g_mlstm[l]), bsz, seq)

        out_c = _mla_attn(qc, kc, vc, bsz, seq)

        moe = (l % 2 == 1)
        router = None
        if moe:
            w_hi = w_router[l // 2].astype(BF16)
            w_lo = (w_router[l // 2] - w_hi.astype(F32)).astype(BF16)
            wr = jnp.pad(jnp.concatenate([w_hi, w_lo], axis=1), ((0, 0), (0, LANES - 2 * N_EXPERTS)))
            br = jnp.pad(b_router[l // 2], (0, LANES - N_EXPERTS)).reshape(1, LANES)
            router = (wr, br)
        res = _mix_out(out_a, out_b, out_c, w_out[l].astype(BF16), x2, mod(0, 2), row1(g_norm[l, 1]),
                       mod(1, 0), mod(1, 1), row1(g_norm[l, 2]), router, bsz, seq)
        if moe:
            x2, h2, ri, rg, counts = res
            x2 = _moe_ffn(h2, ri, rg, counts, w_gate_e[l // 2], w_up_e[l // 2], w_down_e[l // 2],
                          x2, mod(1, 2), row1(g_norm[l, 3]), bsz, seq)
        else:
            x2, h2 = res
            x2 = _ffn_dense(h2, w_gate_d[l // 2], w_up_d[l // 2], w_down_d[l // 2],
                            x2, mod(1, 2), row1(g_norm[l, 3]), bsz, seq)
    return x2.reshape(bsz, seq, d)
```

```python
import functools
import math

import jax
import jax.numpy as jnp
import numpy as np
from jax import lax
from jax.experimental import pallas as pl
from jax.experimental.pallas import tpu as pltpu

F32 = jnp.float32
BF16 = jnp.bfloat16
HIGHEST = lax.Precision.HIGHEST

D_MODEL = 1024
DEPTH = 2
CHUNK = 64
EPS = 1e-6
A_HEADS, A_QK_DIM, A_V_DIM = 4, 32, 64
B_HEADS, B_DIM, CONV_WIDTH = 6, 64, 4
C_HEADS, C_Q_LORA, C_KV_LORA, C_NOPE, C_ROPE, C_V_DIM = 6, 192, 128, 64, 32, 64
ROPE_THETA = 10000.0
D_FF = 3584
N_EXPERTS = 8
LOG2E = 1.4426950408889634
NEG_BIG = -1e30

LANES = 128
DIFF_W = 2 * A_HEADS * A_QK_DIM
MLSTM_W = B_HEADS * B_DIM
MLA_SLAB = 128
MLA_QK_W = C_HEADS * MLA_SLAB
MLA_V_W = C_HEADS * C_V_DIM

COL_DIFF = 0
COL_MLSTM = 768
COL_CQ = 2304
COL_CKV = 2560
COL_X = 2688
COL_Y = 2816
IN_COLS = 2944

VMEM_LIMIT = 56 * 1024 * 1024


def _cparams(sem):
    return pltpu.CompilerParams(dimension_semantics=sem, vmem_limit_bytes=VMEM_LIMIT)


def _nt(a, b):
    return lax.dot_general(a, b, (((1,), (1,)), ((), ())), preferred_element_type=F32)


def _nn(a, b):
    return jnp.dot(a, b, preferred_element_type=F32)


def _sigmoid(v):
    return 1.0 / (1.0 + jnp.exp(-v))


def _rms(v, n):
    return v * lax.rsqrt(jnp.sum(v * v, axis=-1, keepdims=True) * (1.0 / n) + EPS)


def _ada_body(c_ref, w_ref, b_ref, o_ref):
    c = c_ref[...]
    sc = c * _sigmoid(c)
    o_ref[0, 0] = jnp.dot(sc, w_ref[0], precision=HIGHEST, preferred_element_type=F32) + b_ref[0]


def _ada_all(c, w_ada, b_ada):
    bsz, d = c.shape
    n = w_ada.shape[0] * w_ada.shape[1]
    w = w_ada.reshape(n, d, 3 * d)
    b = b_ada.reshape(n, 1, 3 * d)
    return pl.pallas_call(
        _ada_body,
        grid=(n, 3),
        in_specs=[pl.BlockSpec((bsz, d), lambda i, j: (0, 0)),
                  pl.BlockSpec((1, d, d), lambda i, j: (i, 0, j)),
                  pl.BlockSpec((1, 1, d), lambda i, j: (i, 0, j))],
        out_specs=pl.BlockSpec((1, 1, bsz, d), lambda i, j: (i, j, 0, 0)),
        out_shape=jax.ShapeDtypeStruct((n, 3, bsz, d), F32),
        compiler_params=_cparams(("arbitrary", "arbitrary")),
        name="ada_mod",
    )(c, w, b)


def _mix_in_body(tiles_per_seq, tm,
                 x_ref, shift_ref, scale_ref, g_ref, win_ref, wconv_ref, bconv_ref, gq_ref, gkv_ref,
                 wq_ref, wqr_ref, wkv_ref, cos_ref, sin_ref,
                 qd_ref, kd_ref, vd_ref, qm_ref, km_ref, vm_ref, om_ref, gt_ref, qc_ref, kc_ref, vc_ref,
                 carry_ref):
    r = pl.program_id(0)
    x = x_ref[...]
    h = _rms(x, D_MODEL) * g_ref[...]
    h = h * (1.0 + scale_ref[0]) + shift_ref[0]
    p = _nn(h.astype(BF16), win_ref[...])

    qd_ref[...] = (p[:, 0:256] * (A_QK_DIM ** -0.5 * LOG2E)).astype(BF16)
    kd_ref[...] = p[:, 256:512].astype(BF16)
    vd_ref[...] = p[:, 512:768].astype(BF16)

    @pl.when(r % tiles_per_seq == 0)
    def _():
        carry_ref[0:8, :] = jnp.zeros((8, 2 * MLSTM_W), F32)

    carry_ref[8:8 + tm, :] = p[:, COL_MLSTM:COL_MLSTM + 2 * MLSTM_W]
    w = wconv_ref[...]
    conv = bconv_ref[...] + carry_ref[8:8 + tm, :] * w[3:4, :]
    for j in range(CONV_WIDTH - 1):
        conv = conv + carry_ref[5 + j:5 + j + tm, :] * w[j:j + 1, :]
    carry_ref[0:8, :] = carry_ref[tm:tm + 8, :]
    qk = conv * _sigmoid(conv)
    qm_ref[...] = qk[:, 0:MLSTM_W].astype(BF16)
    km_ref[...] = (qk[:, MLSTM_W:2 * MLSTM_W] * (B_DIM ** -0.5)).astype(BF16)
    vm_ref[...] = p[:, COL_MLSTM + 2 * MLSTM_W:COL_MLSTM + 3 * MLSTM_W].astype(BF16)
    om_ref[...] = p[:, COL_MLSTM + 3 * MLSTM_W:COL_MLSTM + 4 * MLSTM_W].astype(BF16)

    xg = p[:, COL_X:COL_X + LANES]
    g16 = xg.T[0:16, :]
    lane_g = lax.broadcasted_iota(jnp.int32, (1, LANES), 1)
    for c2 in range(tm // LANES):
        win = g16[:, LANES * c2:LANES * (c2 + 1)]
        gt_ref[0, 2 * c2] = jnp.where(lane_g < CHUNK, win, 0.0)
        gt_ref[0, 2 * c2 + 1] = jnp.where(lane_g < CHUNK, pltpu.roll(win, CHUNK, 1), 0.0)

    cq = p[:, COL_CQ:COL_CQ + 256]
    cqn = (cq * lax.rsqrt(jnp.sum(cq * cq, axis=-1, keepdims=True) * (1.0 / C_Q_LORA) + EPS) * gq_ref[...]).astype(BF16)
    ckv = p[:, COL_CKV:COL_CKV + C_KV_LORA]
    ckvn = (_rms(ckv, C_KV_LORA) * gkv_ref[...]).astype(BF16)
    cos_t = cos_ref[...]
    sin_t = sin_ref[...]
    cos6 = jnp.concatenate([cos_t] * C_HEADS, axis=1)
    sin6 = jnp.concatenate([sin_t] * C_HEADS, axis=1)
    qn = _nn(cqn, wq_ref[...])
    qr = _nn(cqn, wqr_ref[...])
    qc_ref[...] = ((qn * cos6 + qr * sin6) * ((C_NOPE + C_ROPE) ** -0.5 * LOG2E)).astype(BF16)
    kvp = _nn(ckvn, wkv_ref[...])
    lane = lax.broadcasted_iota(jnp.int32, (1, LANES), 1)
    rope_lanes = (lane >= C_NOPE) & (lane < C_NOPE + C_ROPE)
    yg = p[:, COL_Y:COL_Y + LANES]
    krope = jnp.where(rope_lanes, xg * cos_t + yg * sin_t, 0.0)
    kc_ref[...] = (kvp[:, 0:MLA_QK_W] + jnp.concatenate([krope] * C_HEADS, axis=1)).astype(BF16)
    vc_ref[...] = kvp[:, MLA_QK_W:MLA_QK_W + MLA_V_W].astype(BF16)


def _mix_in(x2, shift, scale, g, win, wconv, bconv, gq, gkv, wq, wqr, wkv, cos_t, sin_t, bsz, seq):
    n = bsz * seq
    tm = min(512, seq)
    tps = seq // tm
    d = D_MODEL
    row = lambda w: pl.BlockSpec((tm, w), lambda r: (r, 0))
    full = lambda a: pl.BlockSpec(a.shape, lambda r: (0,) * a.ndim)
    mod = pl.BlockSpec((1, 1, d), lambda r: (r // tps, 0, 0))
    out_widths = [DIFF_W, DIFF_W, DIFF_W, MLSTM_W, MLSTM_W, MLSTM_W, MLSTM_W]
    out_shape = [jax.ShapeDtypeStruct((n, w), BF16) for w in out_widths]
    out_specs = [row(w) for w in out_widths]
    out_shape.append(jax.ShapeDtypeStruct((bsz, seq // CHUNK, 16, LANES), F32))
    out_specs.append(pl.BlockSpec((1, tm // CHUNK, 16, LANES), lambda r: (r // tps, r % tps, 0, 0)))
    for w in (MLA_QK_W, MLA_QK_W, MLA_V_W):
        out_shape.append(jax.ShapeDtypeStruct((n, w), BF16))
        out_specs.append(row(w))
    return pl.pallas_call(
        functools.partial(_mix_in_body, tps, tm),
        grid=(n // tm,),
        in_specs=[row(d), mod, mod, full(g), full(win), full(wconv), full(bconv), full(gq), full(gkv),
                  full(wq), full(wqr), full(wkv), row(LANES), row(LANES)],
        out_specs=out_specs,
        out_shape=out_shape,
        scratch_shapes=[pltpu.VMEM((tm + 8, 2 * MLSTM_W), F32)],
        compiler_params=_cparams(("arbitrary",)),
        name="mix_in",
    )(x2, shift, scale, g, win, wconv, bconv, gq, gkv, wq, wqr, wkv, cos_t, sin_t)


def _chunk_mask(tq, tk):
    ri = lax.broadcasted_iota(jnp.int32, (tq, tk), 0)
    ci = lax.broadcasted_iota(jnp.int32, (tq, tk), 1)
    return (ci >> 6) <= (ri >> 6)


def _two_pass_softmax(i, nq, tk, score_fn, pv_fn, mask, s_ref, p_ref, mx_ref, l_ref):
    rows = mx_ref.shape[0]
    tq = mask.shape[0]
    mx_ref[...] = jnp.full(mx_ref.shape, NEG_BIG, F32)

    def pass1(j, masked):
        s = score_fn(j)
        if masked:
            s = jnp.where(mask, s.reshape(rows // tq, tq, tk), NEG_BIG).reshape(rows, tk)
        s_ref[j] = s
        mx_ref[...] = jnp.maximum(mx_ref[...], jnp.maximum(s[:, 0:LANES], s[:, LANES:2 * LANES]))

    def body1(jj, c):
        pass1(2 * jj, False)
        pass1(2 * jj + 1, False)
        return c

    lax.fori_loop(0, lax.shift_right_logical(i, 1), body1, 0)

    @pl.when((i & 1) == 1)
    def _():
        pass1(i - 1, False)

    pass1(i, True)
    mx_ref[...] = jnp.broadcast_to(jnp.max(mx_ref[...], axis=1, keepdims=True), mx_ref.shape)

    rb = 128

    def pass2_pv(n):
        for j in range(n):
            for r0 in range(0, rows, rb):
                s = s_ref[j, r0:r0 + rb, :]
                m = mx_ref[r0:r0 + rb, :]
                p0 = jnp.exp2(s[:, 0:LANES] - m)
                p1 = jnp.exp2(s[:, LANES:2 * LANES] - m)
                l_ref[r0:r0 + rb, :] = (p0 + p1) if j == 0 else (l_ref[r0:r0 + rb, :] + (p0 + p1))
                p_ref[j, r0:r0 + rb, :] = jnp.concatenate([p0, p1], axis=1).astype(BF16)
        pv_fn(n)

    for n in range(1, nq + 1):
        @pl.when(i == n - 1)
        def _():
            pass2_pv(n)
    return jnp.sum(l_ref[...], axis=1, keepdims=True)


def _diff_attn_body(tq, nq, lam_init, q_ref, k_ref, v_ref, lam_ref, g_ref, o_ref,
                    qs_ref, s_ref, p_ref, mx_ref, l_ref, acc_ref):
    i = pl.program_id(1)
    tk = tq
    nslab = 2 * A_HEADS
    q = q_ref[...]
    lane = lax.broadcasted_iota(jnp.int32, (1, DIFF_W), 1)
    zero = jnp.zeros_like(q)
    for s in range(nslab):
        qs_ref[s * tq:(s + 1) * tq, :] = jnp.where((lane >= A_QK_DIM * s) & (lane < A_QK_DIM * (s + 1)), q, zero)
    mask = _chunk_mask(tq, tk)

    def score(j):
        return _nt(qs_ref[...], k_ref[pl.ds(pl.multiple_of(j * tk, tk), tk), :])

    def pv(n):
        acc_ref[...] = sum(_nn(p_ref[j], v_ref[j * tk:(j + 1) * tk, :]) for j in range(n))

    l = _two_pass_softmax(i, nq, tk, score, pv, mask, s_ref, p_ref, mx_ref, l_ref)

    lv = lam_ref[...]
    lam = (jnp.exp(jnp.sum(lv[0:1] * lv[1:2], axis=1, keepdims=True))
           - jnp.exp(jnp.sum(lv[2:3] * lv[3:4], axis=1, keepdims=True)) + lam_init)
    o = acc_ref[...] / l
    out = jnp.zeros((tq, DIFF_W), F32)
    for hd in range(A_HEADS):
        sel = (lane >= A_V_DIM * hd) & (lane < A_V_DIM * (hd + 1))
        oh = o[hd * tq:(hd + 1) * tq] - lam * o[(A_HEADS + hd) * tq:(A_HEADS + hd + 1) * tq]
        ms = jnp.sum(jnp.where(sel, oh * oh, 0.0), axis=1, keepdims=True) * (1.0 / A_V_DIM)
        out = out + jnp.where(sel, oh * lax.rsqrt(ms + EPS), 0.0)
    o_ref[...] = (out * g_ref[...] * (1.0 - lam_init)).astype(BF16)


def _diff_attn(qd, kd, vd, lamv, g4, lam_init, bsz, seq):
    tq = min(256, seq)
    nq = seq // tq
    nslab = 2 * A_HEADS
    rows = nslab * tq
    return pl.pallas_call(
        functools.partial(_diff_attn_body, tq, nq, lam_init),
        grid=(bsz, nq),
        in_specs=[pl.BlockSpec((tq, DIFF_W), lambda b, i: (b * nq + i, 0)),
                  pl.BlockSpec((seq, DIFF_W), lambda b, i: (b, 0)),
                  pl.BlockSpec((seq, DIFF_W), lambda b, i: (b, 0)),
                  pl.BlockSpec(lamv.shape, lambda b, i: (0, 0)),
                  pl.BlockSpec(g4.shape, lambda b, i: (0, 0))],
        out_specs=pl.BlockSpec((tq, DIFF_W), lambda b, i: (b * nq + i, 0)),
        out_shape=jax.ShapeDtypeStruct((bsz * seq, DIFF_W), BF16),
        scratch_shapes=[pltpu.VMEM((rows, DIFF_W), BF16), pltpu.VMEM((nq, rows, tq), F32),
                        pltpu.VMEM((nq, rows, tq), BF16), pltpu.VMEM((rows, LANES), F32),
                        pltpu.VMEM((rows, LANES), F32), pltpu.VMEM((rows, DIFF_W), F32)],
        compiler_params=_cparams(("arbitrary", "arbitrary")),
        name="diff_attn",
    )(qd, kd, vd, lamv, g4)


def _mla_attn_body(tq, nq, q_ref, k_ref, v_ref, o_ref, s_ref, p_ref, mx_ref, l_ref, acc1_ref, acc2_ref):
    i = pl.program_id(1)
    tk = tq
    mask = _chunk_mask(tq, tk)
    g1 = 4

    def score(j):
        kt = k_ref[pl.ds(pl.multiple_of(j * tk, tk), tk), :]
        return jnp.concatenate([_nt(q_ref[:, MLA_SLAB * hd:MLA_SLAB * (hd + 1)], kt[:, MLA_SLAB * hd:MLA_SLAB * (hd + 1)])
                                for hd in range(C_HEADS)], axis=0)

    def pv(n):
        acc1_ref[...] = sum(_nn(p_ref[j, 0:g1 * tq, :], v_ref[j * tk:(j + 1) * tk, 0:256]) for j in range(n))
        acc2_ref[...] = sum(_nn(p_ref[j, g1 * tq:, :], v_ref[j * tk:(j + 1) * tk, 256:MLA_V_W]) for j in range(n))

    l = _two_pass_softmax(i, nq, tk, score, pv, mask, s_ref, p_ref, mx_ref, l_ref)

    linv = 1.0 / l
    o1 = acc1_ref[...] * linv[0:g1 * tq]
    o2 = acc2_ref[...] * linv[g1 * tq:]
    lane1 = lax.broadcasted_iota(jnp.int32, (1, 256), 1)
    lane2 = lax.broadcasted_iota(jnp.int32, (1, 128), 1)
    out1 = jnp.zeros((tq, 256), F32)
    for hd in range(g1):
        out1 = out1 + jnp.where((lane1 >= C_V_DIM * hd) & (lane1 < C_V_DIM * (hd + 1)), o1[hd * tq:(hd + 1) * tq], 0.0)
    out2 = jnp.zeros((tq, 128), F32)
    for hd in range(C_HEADS - g1):
        out2 = out2 + jnp.where((lane2 >= C_V_DIM * hd) & (lane2 < C_V_DIM * (hd + 1)), o2[hd * tq:(hd + 1) * tq], 0.0)
    o_ref[...] = jnp.concatenate([out1, out2], axis=1).astype(BF16)


def _mla_attn(qc, kc, vc, bsz, seq):
    tq = min(256, seq)
    nq = seq // tq
    return pl.pallas_call(
        functools.partial(_mla_attn_body, tq, nq),
        grid=(bsz, nq),
        in_specs=[pl.BlockSpec((tq, MLA_QK_W), lambda b, i: (b * nq + i, 0)),
                  pl.BlockSpec((seq, MLA_QK_W), lambda b, i: (b, 0)),
                  pl.BlockSpec((seq, MLA_V_W), lambda b, i: (b, 0))],
        out_specs=pl.BlockSpec((tq, MLA_V_W), lambda b, i: (b * nq + i, 0)),
        out_shape=jax.ShapeDtypeStruct((bsz * seq, MLA_V_W), BF16),
        scratch_shapes=[pltpu.VMEM((nq, C_HEADS * tq, tq), F32), pltpu.VMEM((nq, C_HEADS * tq, tq), BF16),
                        pltpu.VMEM((C_HEADS * tq, LANES), F32), pltpu.VMEM((C_HEADS * tq, LANES), F32),
                        pltpu.VMEM((4 * tq, 256), F32), pltpu.VMEM((2 * tq, 128), F32)],
        compiler_params=_cparams(("arbitrary", "arbitrary")),
        name="mla_attn",
    )(qc, kc, vc)


def _mlstm_body(nc, q_ref, k_ref, v_ref, og_ref, gates_ref, gbias_ref, gout_ref, o_ref,
                ct_ref, m_ref, a_ref, b_ref, al_ref, bl_ref):
    ct_ref[...] = jnp.zeros(ct_ref.shape, F32)
    m_ref[...] = jnp.zeros(m_ref.shape, F32)
    lane = lax.broadcasted_iota(jnp.int32, (1, LANES), 1)
    ti = lax.broadcasted_iota(jnp.int32, (CHUNK, CHUNK), 0)
    si = lax.broadcasted_iota(jnp.int32, (CHUNK, CHUNK), 1)
    tril = si <= ti
    eye = si == ti
    eye128 = (lax.broadcasted_iota(jnp.int32, (LANES, LANES), 0)
              == lax.broadcasted_iota(jnp.int32, (LANES, LANES), 1)).astype(BF16)
    gbias = gbias_ref[...]
    gout = gout_ref[...]

    g = gates_ref[0]
    ig = (g[:, 0:8, :] + gbias[0:8]).reshape(nc * 8, LANES)
    fg = (g[:, 8:16, :] + gbias[8:16]).reshape(nc * 8, LANES)
    lf = jnp.minimum(fg, 0.0) - jnp.log(1.0 + jnp.exp(-jnp.abs(fg)))
    b_all = jnp.where(lane < CHUNK, lf, 0.0)
    for sh in (1, 2, 4, 8, 16, 32):
        b_all = b_all + jnp.where(lane >= sh, pltpu.roll(b_all, sh, 1), 0.0)
    a_all = ig - b_all
    a_ref[...] = a_all.reshape(nc, 8, LANES)
    b_ref[...] = b_all.reshape(nc, 8, LANES)
    al_ref[...] = jnp.broadcast_to(jnp.max(jnp.where(lane < CHUNK, a_all, NEG_BIG), axis=1, keepdims=True),
                                   (nc * 8, LANES)).reshape(nc, 8, LANES)
    bl_ref[...] = jnp.broadcast_to(jnp.sum(jnp.where(lane == CHUNK - 1, b_all, 0.0), axis=1, keepdims=True),
                                   (nc * 8, LANES)).reshape(nc, 8, LANES)

    def chunk(c, carry):
        rows = pl.ds(pl.multiple_of(c * CHUNK, CHUNK), CHUNK)
        a = a_ref[c]
        b = b_ref[c]
        b_last = bl_ref[c]

        for pr in range(B_HEADS // 2):
            cols = slice(LANES * pr, LANES * (pr + 1))
            qw = q_ref[rows, cols]
            kw = k_ref[rows, cols]
            vw = v_ref[rows, cols]
            kt = _nt(eye128, kw)
            ctp = ct_ref[pr]
            ctb = ctp.astype(BF16)
            outs = []
            for hh in range(2):
                hd = 2 * pr + hh
                half = (lane < CHUNK) if hh == 0 else (lane >= CHUNK)
                onecol = (lane == CHUNK) if hh == 0 else (lane == 0)
                qmk = jnp.where(half, qw, jnp.zeros_like(qw))
                vaug = jnp.where(half, vw, jnp.where(onecol, 1.0, 0.0).astype(BF16))
                a_h = a[hd:hd + 1, 0:CHUNK]
                b_h = b[hd:hd + 1, 0:CHUNK]
                full = (CHUNK, LANES)
                a_col = jnp.broadcast_to(jnp.max(jnp.where(tril, a_h, NEG_BIG), axis=1, keepdims=True), full)
                b_col = jnp.broadcast_to(jnp.sum(jnp.where(eye, b_h, 0.0), axis=1, keepdims=True), full)
                w0 = jnp.where(tril, jnp.exp(a_h - a_col[:, 0:CHUNK]), 0.0)
                sc0 = (_nt(qmk, kw) * w0).astype(BF16)
                x0 = _nn(sc0, vaug)
                xi = _nn(qmk, ctb)
                m = m_ref[hd:hd + 1, :]
                m_col = jnp.maximum(a_col, m)
                xs = jnp.exp(a_col - m_col) * x0 + jnp.exp(m - m_col) * xi
                den = jnp.broadcast_to(jnp.sum(jnp.where(onecol, xs, 0.0), axis=1, keepdims=True), full)
                hv = xs / jnp.maximum(jnp.abs(den), jnp.exp(-b_col - m_col))
                ms = jnp.broadcast_to(jnp.sum(jnp.where(half, hv * hv, 0.0), axis=1, keepdims=True), full)
                outs.append(hv * lax.rsqrt(ms * (1.0 / B_DIM) + EPS))
                al = al_ref[c][hd:hd + 1, :]
                m_l = jnp.maximum(al, m)
                kwt = (kt[CHUNK * hh:CHUNK * (hh + 1), :] * jnp.exp(a_h - al[:, 0:CHUNK])).astype(BF16)
                dct = _nn(kwt, vaug)
                ct_ref[pr, CHUNK * hh:CHUNK * (hh + 1), :] = (jnp.exp(m - m_l) * ctp[CHUNK * hh:CHUNK * (hh + 1)]
                                                              + jnp.exp(al - m_l) * dct)
                m_ref[hd:hd + 1, :] = b_last[hd:hd + 1, :] + m_l
            hn = jnp.where(lane < CHUNK, outs[0], outs[1])
            og = og_ref[rows, cols].astype(F32)
            o_ref[rows, cols] = (hn * gout[:, cols] * _sigmoid(og)).astype(BF16)
        return carry

    lax.fori_loop(0, nc, chunk, 0, unroll=2)


def _mlstm(qm, km, vm, om, gates, gbias, gout, bsz, seq):
    nc = seq // CHUNK
    blk = pl.BlockSpec((seq, MLSTM_W), lambda b: (b, 0))
    return pl.pallas_call(
        functools.partial(_mlstm_body, nc),
        grid=(bsz,),
        in_specs=[blk, blk, blk, blk,
                  pl.BlockSpec((1, nc, 16, LANES), lambda b: (b, 0, 0, 0)),
                  pl.BlockSpec(gbias.shape, lambda b: (0, 0)),
                  pl.BlockSpec(gout.shape, lambda b: (0, 0))],
        out_specs=blk,
        out_shape=jax.ShapeDtypeStruct((bsz * seq, MLSTM_W), BF16),
        scratch_shapes=[pltpu.VMEM((B_HEADS // 2, LANES, LANES), F32), pltpu.VMEM((8, LANES), F32)]
        + [pltpu.VMEM((nc, 8, LANES), F32)] * 4,
        compiler_params=_cparams(("arbitrary",)),
        name="mlstm",
    )(qm, km, vm, om, gates, gbias, gout)


def _mix_out_body(with_router, tm,
                  oa_ref, ob_ref, oc_ref, wout_ref, x_ref, gate_ref, g1_ref, shift_ref, scale_ref, g2_ref, *rest):
    if with_router:
        wr_ref, br_ref, xo_ref, h_ref, ri_ref, rg_ref, cnt_ref, run_ref = rest
    else:
        xo_ref, h_ref = rest
    na = A_HEADS * A_V_DIM
    y = (_nn(oa_ref[...], wout_ref[0:na, :]) + _nn(ob_ref[...], wout_ref[na:na + MLSTM_W, :])
         + _nn(oc_ref[...], wout_ref[na + MLSTM_W:, :]))
    xn = x_ref[...] + gate_ref[0] * (_rms(y, D_MODEL) * g1_ref[...])
    xo_ref[...] = xn
    h = _rms(xn, D_MODEL) * g2_ref[...]
    h = h * (1.0 + scale_ref[0]) + shift_ref[0]
    h_ref[...] = h.astype(h_ref.dtype)
    if with_router:
        @pl.when(pl.program_id(0) == 0)
        def _():
            run_ref[...] = jnp.zeros(run_ref.shape, F32)

        lane = lax.broadcasted_iota(jnp.int32, (1, LANES), 1)
        h_hi = h.astype(BF16)
        h_lo = (h - h_hi.astype(F32)).astype(BF16)
        part = _nn(h_hi, wr_ref[...]) + _nn(h_lo, wr_ref[...])
        logits = part + pltpu.roll(part, LANES - N_EXPERTS, 1) + br_ref[...]
        lg = jnp.where(lane < N_EXPERTS, logits, NEG_BIG)
        v1 = jnp.max(lg, axis=1, keepdims=True)
        i1 = jnp.min(jnp.where(lg == v1, lane, LANES), axis=1, keepdims=True)
        lg2 = jnp.where(lane == i1, NEG_BIG, lg)
        v2 = jnp.max(lg2, axis=1, keepdims=True)
        i2 = jnp.min(jnp.where(lg2 == v2, lane, LANES), axis=1, keepdims=True)
        gt1 = 1.0 / (1.0 + jnp.exp(v2 - v1))
        oh1 = lane == i1
        oh2 = lane == i2
        cnt = jnp.where(oh1 | oh2, 1.0, 0.0)
        ri_ = lax.broadcasted_iota(jnp.int32, (tm, tm), 0)
        ci_ = lax.broadcasted_iota(jnp.int32, (tm, tm), 1)
        lower = jnp.where(ci_ < ri_, 1.0, 0.0).astype(BF16)
        before = _nn(lower, cnt.astype(BF16)) + run_ref[0:1, :]
        r1 = jnp.sum(jnp.where(oh1, before, 0.0), axis=1, keepdims=True)
        r2 = jnp.sum(jnp.where(oh2, before, 0.0), axis=1, keepdims=True)
        run_ref[...] = run_ref[...] + jnp.sum(cnt, axis=0, keepdims=True)
        ri_ref[...] = jnp.where(lane == 0, i1, jnp.where(lane == 1, i2, jnp.where(
            lane == 2, r1.astype(jnp.int32), jnp.where(lane == 3, r2.astype(jnp.int32), 0))))
        rg_ref[...] = jnp.where(lane == 0, gt1, jnp.where(lane == 1, 1.0 - gt1, 0.0))
        cnt_ref[...] = run_ref[...]


def _mix_out(oa, ob, oc, wout, x2, gate, g1, shift, scale, g2, router, bsz, seq):
    n = bsz * seq
    d = D_MODEL
    tm = min(512, seq)
    tps = seq // tm
    row = lambda w: pl.BlockSpec((tm, w), lambda r: (r, 0))
    full = lambda a: pl.BlockSpec(a.shape, lambda r: (0,) * a.ndim)
    mod = pl.BlockSpec((1, 1, d), lambda r: (r // tps, 0, 0))
    in_specs = [row(oa.shape[1]), row(ob.shape[1]), row(oc.shape[1]), full(wout), row(d), mod, full(g1), mod, mod, full(g2)]
    args = [oa, ob, oc, wout, x2, gate, g1, shift, scale, g2]
    out_shape = [jax.ShapeDtypeStruct((n, d), F32), jax.ShapeDtypeStruct((n, d), BF16 if router is None else F32)]
    out_specs = [row(d), row(d)]
    scratch = []
    if router is not None:
        wr, br = router
        in_specs += [full(wr), full(br)]
        args += [wr, br]
        out_shape += [jax.ShapeDtypeStruct((n, LANES), jnp.int32), jax.ShapeDtypeStruct((n, LANES), F32),
                      jax.ShapeDtypeStruct((8, LANES), F32)]
        out_specs += [row(LANES), row(LANES), pl.BlockSpec((8, LANES), lambda r: (0, 0))]
        scratch = [pltpu.VMEM((8, LANES), F32)]
    return pl.pallas_call(
        functools.partial(_mix_out_body, router is not None, tm),
        grid=(n // tm,),
        in_specs=in_specs,
        out_specs=out_specs,
        out_shape=out_shape,
        scratch_shapes=scratch,
        compiler_params=_cparams(("arbitrary",)),
        name="mix_out_router" if router is not None else "mix_out",
    )(*args)


def _ffn_dense_body(h_ref, wg_ref, wu_ref, wd_ref, x_ref, gate_ref, g_ref, o_ref, acc_ref):
    j = pl.program_id(1)

    @pl.when(j == 0)
    def _():
        acc_ref[...] = jnp.zeros(acc_ref.shape, F32)

    h = h_ref[...]
    a = _nn(h, wg_ref[...].astype(BF16))
    u = _nn(h, wu_ref[...].astype(BF16))
    acc_ref[...] += _nn((a * _sigmoid(a) * u).astype(BF16), wd_ref[...].astype(BF16))

    @pl.when(j == pl.num_programs(1) - 1)
    def _():
        o_ref[...] = x_ref[...] + gate_ref[0] * (_rms(acc_ref[...], D_MODEL) * g_ref[...])


def _ffn_dense(h, wg, wu, wd, x2, gate, g, bsz, seq):
    n = bsz * seq
    d = D_MODEL
    tm = min(1024, seq)
    tps = seq // tm
    tf = 512
    return pl.pallas_call(
        _ffn_dense_body,
        grid=(n // tm, D_FF // tf),
        in_specs=[pl.BlockSpec((tm, d), lambda i, j: (i, 0)),
                  pl.BlockSpec((d, tf), lambda i, j: (0, j)),
                  pl.BlockSpec((d, tf), lambda i, j: (0, j)),
                  pl.BlockSpec((tf, d), lambda i, j: (j, 0)),
                  pl.BlockSpec((tm, d), lambda i, j: (i, 0)),
                  pl.BlockSpec((1, 1, d), lambda i, j: (i // tps, 0, 0)),
                  pl.BlockSpec(g.shape, lambda i, j: (0, 0))],
        out_specs=pl.BlockSpec((tm, d), lambda i, j: (i, 0)),
        out_shape=jax.ShapeDtypeStruct((n, d), F32),
        scratch_shapes=[pltpu.VMEM((tm, d), F32)],
        compiler_params=_cparams(("arbitrary", "arbitrary")),
        name="ffn_dense",
    )(h, wg, wu, wd, x2, gate, g)


def _ffn_expert_body(blk_e_ref, nvalid_ref, x_ref, wg_ref, wu_ref, wd_ref, o_ref, acc_ref):
    i = pl.program_id(0)
    j = pl.program_id(1)
    valid = i < nvalid_ref[0]

    @pl.when(j == 0)
    def _():
        acc_ref[...] = jnp.zeros(acc_ref.shape, F32)

    @pl.when(valid)
    def _():
        xb = x_ref[...].astype(BF16)
        a = _nn(xb, wg_ref[0].astype(BF16))
        u = _nn(xb, wu_ref[0].astype(BF16))
        acc_ref[...] += _nn((a * _sigmoid(a) * u).astype(BF16), wd_ref[0].astype(BF16))

    @pl.when(j == pl.num_programs(1) - 1)
    def _():
        o_ref[...] = acc_ref[...]


def _ffn_expert(xb, wg, wu, wd, blk_e, nvalid, tmb):
    p, d = xb.shape
    nb = p // tmb
    tf = 512
    nf = D_FF // tf

    def jeff(i, j, nv):
        return jnp.where(i < nv[0], j, nf - 1)

    grid_spec = pltpu.PrefetchScalarGridSpec(
        num_scalar_prefetch=2,
        grid=(nb, nf),
        in_specs=[pl.BlockSpec((tmb, d), lambda i, j, be, nv: (i, 0)),
                  pl.BlockSpec((1, d, tf), lambda i, j, be, nv: (be[i], 0, jeff(i, j, nv))),
                  pl.BlockSpec((1, d, tf), lambda i, j, be, nv: (be[i], 0, jeff(i, j, nv))),
                  pl.BlockSpec((1, tf, d), lambda i, j, be, nv: (be[i], jeff(i, j, nv), 0))],
        out_specs=pl.BlockSpec((tmb, d), lambda i, j, be, nv: (i, 0)),
        scratch_shapes=[pltpu.VMEM((tmb, d), F32)],
    )
    return pl.pallas_call(
        _ffn_expert_body,
        grid_spec=grid_spec,
        out_shape=jax.ShapeDtypeStruct((p, d), F32),
        compiler_params=_cparams(("arbitrary", "arbitrary")),
        name="ffn_expert",
    )(blk_e, nvalid, xb, wg, wu, wd)


DISPATCH_TOKENS = 512


def _moe_dispatch_body(dest_ref, h_ref, xb_in_ref, xb_ref, sem):
    del xb_in_ref

    def row_copy(t, k):
        return pltpu.make_async_copy(h_ref.at[pl.ds(t, 1), :], xb_ref.at[pl.ds(dest_ref[2 * t + k], 1), :], sem)

    def issue(t, c):
        row_copy(t, 0).start(priority=0)
        row_copy(t, 1).start(priority=1)
        return c

    def drain(t, c):
        row_copy(t, 0).wait()
        row_copy(t, 1).wait()
        return c

    lax.fori_loop(0, DISPATCH_TOKENS, issue, 0)
    lax.fori_loop(0, DISPATCH_TOKENS, drain, 0)


def _moe_dispatch(h, dest, p):
    n, d = h.shape
    td = DISPATCH_TOKENS
    return pl.pallas_call(
        _moe_dispatch_body,
        grid=(n // td,),
        in_specs=[pl.BlockSpec((2 * td,), lambda i: (i,), memory_space=pltpu.SMEM),
                  pl.BlockSpec((td, d), lambda i: (i, 0)),
                  pl.BlockSpec(memory_space=pl.ANY)],
        out_specs=pl.BlockSpec(memory_space=pl.ANY),
        out_shape=jax.ShapeDtypeStruct((p, d), h.dtype),
        scratch_shapes=[pltpu.SemaphoreType.DMA(())],
        input_output_aliases={2: 0},
        compiler_params=_cparams(("arbitrary",)),
        name="moe_dispatch",
    )(dest, h, jnp.zeros((p, d), h.dtype))


def _moe_combine_body(y1_ref, y2_ref, rg_ref, x_ref, gate_ref, g_ref, o_ref):
    rg = rg_ref[...]
    y = rg[:, 0:1] * y1_ref[...] + rg[:, 1:2] * y2_ref[...]
    o_ref[...] = x_ref[...] + gate_ref[0] * (_rms(y, D_MODEL) * g_ref[...])


def _moe_combine(y1, y2, rg, x2, gate, g, bsz, seq):
    n = bsz * seq
    d = D_MODEL
    tm = min(512, seq)
    tps = seq // tm
    row = lambda w: pl.BlockSpec((tm, w), lambda r: (r, 0))
    return pl.pallas_call(
        _moe_combine_body,
        grid=(n // tm,),
        in_specs=[row(d), row(d), row(LANES), row(d), pl.BlockSpec((1, 1, d), lambda r: (r // tps, 0, 0)),
                  pl.BlockSpec(g.shape, lambda r: (0, 0))],
        out_specs=row(d),
        out_shape=jax.ShapeDtypeStruct((n, d), F32),
        compiler_params=_cparams(("arbitrary",)),
        name="moe_combine",
    )(y1, y2, rg, x2, gate, g)


def _moe_ffn(h, ri, rg, counts, wg, wu, wd, x2, gate, g, bsz, seq):
    n = bsz * seq
    tmb = min(1024, n // 8)
    cnt = counts[0, :N_EXPERTS].astype(jnp.int32)
    padded = ((cnt + tmb - 1) // tmb) * tmb
    pend = jnp.cumsum(padded)
    pstart = pend - padded
    dest1 = pstart[ri[:, 0]] + ri[:, 2]
    dest2 = pstart[ri[:, 1]] + ri[:, 3]
    nb = (2 * n) // tmb + N_EXPERTS
    p = nb * tmb
    nvalid = (pend[-1] // tmb).astype(jnp.int32).reshape(1)
    blk = jnp.minimum(jnp.arange(nb, dtype=jnp.int32), nvalid[0] - 1) * tmb
    blk_e = jnp.minimum(jnp.sum(blk[:, None] >= pend[None, :], axis=1), N_EXPERTS - 1).astype(jnp.int32)
    xb = _moe_dispatch(h, jnp.stack([dest1, dest2], axis=1).reshape(2 * n), p)
    yb = _ffn_expert(xb, wg, wu, wd, blk_e, nvalid, tmb)
    y1 = jnp.take(yb, dest1, axis=0)
    y2 = jnp.take(yb, dest2, axis=0)
    return _moe_combine(y1, y2, rg, x2, gate, g, bsz, seq)


def _rot_cols(w):
    half = C_ROPE // 2
    return jnp.concatenate([-w[:, half:], w[:, :half]], axis=1)


def _prep_w_in(w_in):
    d = w_in.shape[0]
    z = lambda k: jnp.zeros((d, k), w_in.dtype)
    gates = w_in[:, 2304:2316]
    cq = w_in[:, 2316:2508]
    ckv = w_in[:, 2508:2636]
    kr = w_in[:, 2636:2668]
    cols = [w_in[:, :2304], cq, z(64), ckv,
            gates[:, 0:B_HEADS], z(2), gates[:, B_HEADS:], z(C_NOPE - 2 * B_HEADS - 2), kr, z(LANES - C_NOPE - C_ROPE),
            z(C_NOPE), _rot_cols(kr), z(LANES - C_NOPE - C_ROPE)]
    out = jnp.concatenate(cols, axis=1).astype(BF16)
    assert out.shape[1] == IN_COLS
    return out


def _prep_w_uq(w_uq):
    w = w_uq.reshape(C_Q_LORA, C_HEADS, C_NOPE + C_ROPE)
    nope, rp = w[..., :C_NOPE], w[..., C_NOPE:]
    pad = jnp.zeros((C_Q_LORA, C_HEADS, MLA_SLAB - C_NOPE - C_ROPE), w.dtype)
    rot = jnp.concatenate([-rp[..., C_ROPE // 2:], rp[..., :C_ROPE // 2]], axis=-1)
    wq = jnp.concatenate([nope, rp, pad], axis=-1).reshape(C_Q_LORA, MLA_QK_W)
    wqr = jnp.concatenate([jnp.zeros_like(nope), rot, pad], axis=-1).reshape(C_Q_LORA, MLA_QK_W)
    rowpad = jnp.zeros((256 - C_Q_LORA, MLA_QK_W), w.dtype)
    return (jnp.concatenate([wq, rowpad], axis=0).astype(BF16), jnp.concatenate([wqr, rowpad], axis=0).astype(BF16))


def _prep_w_ukv(w_ukv):
    w = w_ukv.reshape(C_KV_LORA, C_HEADS, C_NOPE + C_V_DIM)
    kn = jnp.concatenate([w[..., :C_NOPE], jnp.zeros((C_KV_LORA, C_HEADS, MLA_SLAB - C_NOPE), w.dtype)], axis=-1)
    v = w[..., C_NOPE:]
    return jnp.concatenate([kn.reshape(C_KV_LORA, MLA_QK_W), v.reshape(C_KV_LORA, MLA_V_W)], axis=1).astype(BF16)


def _rope_tables(positions):
    freqs = ROPE_THETA ** (-jnp.arange(0, C_ROPE, 2, dtype=F32) / C_ROPE)
    tail = MLA_SLAB - C_NOPE - C_ROPE
    f128 = jnp.concatenate([jnp.zeros((C_NOPE,), F32), freqs, freqs, jnp.zeros((tail,), F32)])
    ang = positions.astype(F32).reshape(-1, 1) * f128[None, :]
    return jnp.cos(ang), jnp.sin(ang)


def kernel(x, c, positions, w_ada, b_ada, g_norm, w_in, w_out, lam_q1, lam_k1, lam_q2, lam_k2, g_diff, w_conv, b_conv, b_igate, b_fgate, g_mlstm, g_q_lat, g_kv_lat, w_uq, w_ukv, w_gate_d, w_up_d, w_down_d, w_router, b_router, w_gate_e, w_up_e, w_down_e):
    bsz, seq, d = x.shape
    n = bsz * seq
    nc = seq // CHUNK
    depth = w_in.shape[0]
    mods = _ada_all(c, w_ada, b_ada)
    cos_t, sin_t = _rope_tables(positions)
    x2 = x.reshape(n, d)
    row1 = lambda v: v.reshape(1, -1).astype(F32)

    for l in range(depth):
        mod = lambda k, j: mods[2 * l + k, j].reshape(bsz, 1, d)
        wq, wqr = _prep_w_uq(w_uq[l])
        gq = jnp.concatenate([g_q_lat[l], jnp.zeros((256 - C_Q_LORA,), F32)]).reshape(1, 256)
        (qd, kd, vd, qm, km, vm, om, gt, qc, kc, vc) = _mix_in(
            x2, mod(0, 0), mod(0, 1), row1(g_norm[l, 0]), _prep_w_in(w_in[l]), w_conv[l], row1(b_conv[l]),
            gq, row1(g_kv_lat[l]), wq, wqr, _prep_w_ukv(w_ukv[l]), cos_t, sin_t, bsz, seq)

        lam_init = 0.8 - 0.6 * math.exp(-0.3 * l)
        lamv = jnp.zeros((8, LANES), F32).at[0:4, 0:A_QK_DIM].set(jnp.stack([lam_q1[l], lam_k1[l], lam_q2[l], lam_k2[l]]))
        out_a = _diff_attn(qd, kd, vd, lamv, row1(jnp.tile(g_diff[l], A_HEADS)), lam_init, bsz, seq)

        gates = gt
        zb = jnp.zeros((2,), F32)
        gbias = jnp.broadcast_to(jnp.concatenate([b_igate[l], zb, b_fgate[l], zb])[:, None], (16, LANES)).astype(F32)
        out_b = _mlstm(qm, km, vm, om, gates, gbias, row1(g_mlstm[l]), bsz, seq)

        out_c = _mla_attn(qc, kc, vc, bsz, seq)

        moe = (l % 2 == 1)
        router = None
        if moe:
            w_hi = w_router[l // 2].astype(BF16)
            w_lo = (w_router[l // 2] - w_hi.astype(F32)).astype(BF16)
            wr = jnp.pad(jnp.concatenate([w_hi, w_lo], axis=1), ((0, 0), (0, LANES - 2 * N_EXPERTS)))
            br = jnp.pad(b_router[l // 2], (0, LANES - N_EXPERTS)).reshape(1, LANES)
            router = (wr, br)
        res = _mix_out(out_a, out_b, out_c, w_out[l].astype(BF16), x2, mod(0, 2), row1(g_norm[l, 1]),
                       mod(1, 0), mod(1, 1), row1(g_norm[l, 2]), router, bsz, seq)
        if moe:
            x2, h2, ri, rg, counts = res
            x2 = _moe_ffn(h2, ri, rg, counts, w_gate_e[l // 2], w_up_e[l // 2], w_down_e[l // 2],
                          x2, mod(1, 2), row1(g_norm[l, 3]), bsz, seq)
        else:
            x2, h2 = res
            x2 = _ffn_dense(h2, w_gate_d[l // 2], w_up_d[l // 2], w_down_d[l // 2],
                            x2, mod(1, 2), row1(g_norm[l, 3]), bsz, seq)
    return x2.reshape(bsz, seq, d)
```
